```python
import math
import jax, jax.numpy as jnp
from jax import lax
import numpy as np

D_MODEL = 2048
BATCH = 16
SEQ = 256
DEPTH = 2
DEC_BATCH = 8
DEC_SEQ = 2048
PAST_LEN = 512

GRID_W = 64
BLK = 128
EPS = 1e-6
ROPE_BASE = 10000.0
NEG_INF = -1e30
N_HEADS_A = 16
N_KV_A = 2
HD_A = 64
WINDOW = 128
A_W = N_HEADS_A * HD_A
KV_W = N_KV_A * HD_A
N_HEADS_B = 8
Q_LORA = 512
KV_LORA = 256
QK_NOPE = 128
QK_ROPE = 64
V_HD = 128
B_W = N_HEADS_B * V_HD
POOL_WINDOWS = (2, 4, 8, 16)
POOL_GROUPS = 4
POOL_GW = 256
C_W = POOL_GROUPS * POOL_GW
N_BRANCH = 3
OFF_K = A_W
OFF_V = OFF_K + KV_W
OFF_CQ = OFF_V + KV_W
OFF_CKV = OFF_CQ + Q_LORA
OFF_KR = OFF_CKV + KV_LORA
OFF_POOL = OFF_KR + QK_ROPE
OFF_GATE = OFF_POOL + C_W
IN_COLS = OFF_GATE + N_BRANCH * D_MODEL
SPLITS = (OFF_K, OFF_V, OFF_CQ, OFF_CKV, OFF_KR, OFF_POOL, OFF_GATE)
D_FF = 5632
N_EXPERTS = 8
TOP_K = 2
D_FF_E = 5632

kernel_name = "hybrid_diffusion_prefix_trunk_step"


def rmsnorm(x, g):
    xf = x.astype(jnp.float32)
    y = xf * lax.rsqrt(jnp.mean(xf * xf, axis=-1, keepdims=True) + EPS)
    return (y * g.astype(jnp.float32)).astype(x.dtype)


def adaln(cond, w, b):
    m = jnp.dot(jax.nn.silu(cond), w) + b
    return jnp.split(m[:, None, :], 6, axis=-1)


def modulate(h, shift, scale):
    return h * (1 + scale) + shift


def axial_angles(rows, dim):
    quarter = dim // 4
    inv = ROPE_BASE ** (-jnp.arange(quarter, dtype=jnp.float32) / quarter)
    r = jnp.repeat(jnp.arange(rows, dtype=jnp.float32), GRID_W)
    col = jnp.tile(jnp.arange(GRID_W, dtype=jnp.float32), rows)
    return r[:, None] * inv, col[:, None] * inv


def _rot_half(x, ang):
    x1, x2 = jnp.split(x, 2, axis=-1)
    cos = jnp.cos(ang)[:, None, :]
    sin = jnp.sin(ang)[:, None, :]
    return jnp.concatenate([x1 * cos - x2 * sin, x1 * sin + x2 * cos], axis=-1)


def rope2d(x, ang):
    ang_r, ang_c = ang
    xf = x.astype(jnp.float32)
    h = x.shape[-1] // 2
    y = jnp.concatenate([_rot_half(xf[..., :h], ang_r), _rot_half(xf[..., h:], ang_c)], axis=-1)
    return y.astype(x.dtype)


def ctx_attn(q, k, v, sink):
    B_, n, H_, d_ = q.shape
    G_ = k.shape[2]
    R_ = H_ // G_
    nb = n // BLK
    qb = jnp.moveaxis(q.reshape(B_, nb, BLK, G_, R_, d_), 1, 0)
    sk = sink.astype(jnp.float32).reshape(1, G_, R_, 1, 1)
    scale = d_ ** -0.5
    vf = v.astype(jnp.float32)

    def blk(qi):
        s = jnp.einsum('bqgrd,bkgd->bgrqk', qi, k, preferred_element_type=jnp.float32) * scale
        m = jnp.maximum(s.max(-1, keepdims=True), sk)
        p = jnp.exp(s - m)
        denom = p.sum(-1, keepdims=True) + jnp.exp(sk - m)
        return jnp.einsum('bgrqk,bkgd->bqgrd', p / denom, vf).astype(q.dtype)

    o = lax.map(blk, qb)
    return jnp.moveaxis(o, 0, 1).reshape(B_, n, H_ * d_)


def _band_blocks(t, nb):
    B_, N_, G_, d_ = t.shape
    tp = jnp.pad(t, ((0, 0), (BLK, BLK), (0, 0), (0, 0))).reshape(B_, nb + 2, BLK, G_, d_)
    band = jnp.concatenate([tp[:, :-2], tp[:, 1:-1], tp[:, 2:]], axis=2)
    return jnp.moveaxis(band, 1, 0)


def local_ctx_attn(q, k, v, kc, vc, sink):
    B_, N_, H_, d_ = q.shape
    G_ = k.shape[2]
    R_ = H_ // G_
    nb = N_ // BLK
    qb = jnp.moveaxis(q.reshape(B_, nb, BLK, G_, R_, d_), 1, 0)
    kb = _band_blocks(k, nb)
    vb = _band_blocks(v, nb)
    qpos = jnp.arange(N_).reshape(nb, BLK)
    kpos = (jnp.arange(nb)[:, None] - 1) * BLK + jnp.arange(3 * BLK)[None, :]
    valid = ((kpos[:, None, :] >= 0) & (kpos[:, None, :] < N_)
             & (jnp.abs(qpos[:, :, None] - kpos[:, None, :]) <= WINDOW))
    sk = sink.astype(jnp.float32).reshape(1, G_, R_, 1, 1)
    scale = d_ ** -0.5
    vcf = vc.astype(jnp.float32)

    def blk(args):
        qi, ki, vi, mi = args
        s_loc = jnp.einsum('bqgrd,bkgd->bgrqk', qi, ki, preferred_element_type=jnp.float32) * scale
        s_loc = jnp.where(mi[None, None, None], s_loc, NEG_INF)
        s_ctx = jnp.einsum('bqgrd,bkgd->bgrqk', qi, kc, preferred_element_type=jnp.float32) * scale
        m = jnp.maximum(jnp.maximum(s_loc.max(-1, keepdims=True), s_ctx.max(-1, keepdims=True)), sk)
        p_loc = jnp.exp(s_loc - m)
        p_ctx = jnp.exp(s_ctx - m)
        denom = p_loc.sum(-1, keepdims=True) + p_ctx.sum(-1, keepdims=True) + jnp.exp(sk - m)
        o = (jnp.einsum('bgrqk,bkgd->bqgrd', p_loc / denom, vi.astype(jnp.float32))
             + jnp.einsum('bgrqk,bkgd->bqgrd', p_ctx / denom, vcf))
        return o.astype(q.dtype)

    o = lax.map(blk, (qb, kb, vb, valid))
    return jnp.moveaxis(o, 0, 1).reshape(B_, N_, H_ * d_)


def mla_attn(q_nope, q_rope, k_nope, k_rope, v):
    B_, Nq, H_, dn = q_nope.shape
    dr = q_rope.shape[-1]
    nb = Nq // BLK
    qn = jnp.moveaxis(q_nope.reshape(B_, nb, BLK, H_, dn), 1, 0)
    qr = jnp.moveaxis(q_rope.reshape(B_, nb, BLK, H_, dr), 1, 0)
    scale = (dn + dr) ** -0.5
    vf = v.astype(jnp.float32)

    def blk(args):
        qn_i, qr_i = args
        s = (jnp.einsum('bqhd,bkhd->bhqk', qn_i, k_nope, preferred_element_type=jnp.float32)
             + jnp.einsum('bqhd,bkd->bhqk', qr_i, k_rope, preferred_element_type=jnp.float32)) * scale
        p = jax.nn.softmax(s, axis=-1)
        return jnp.einsum('bhqk,bkhd->bqhd', p, vf).astype(q_nope.dtype)

    o = lax.map(blk, (qn, qr))
    return jnp.moveaxis(o, 0, 1).reshape(B_, Nq, H_ * v.shape[-1])


def pool_mixer(u, w, scale):
    B_, n, _ = u.shape
    ug = u.reshape(B_, n, POOL_GROUPS, POOL_GW).astype(jnp.float32)
    cs = jnp.pad(jnp.cumsum(ug, axis=1), ((0, 0), (1, 0), (0, 0), (0, 0)))
    t = jnp.arange(n)
    outs = []
    for g, win in enumerate(POOL_WINDOWS):
        left = win // 2
        right = win - left - 1
        lo = jnp.maximum(t - left, 0)
        hi = jnp.minimum(t + right, n - 1) + 1
        cg = cs[:, :, g]
        cnt = (hi - lo).astype(jnp.float32)[None, :, None]
        outs.append((cg[:, hi] - cg[:, lo]) / cnt - ug[:, :, g])
    d = jnp.stack(outs, axis=2)
    y = jnp.einsum('bngc,gcd->bngd', d, w.astype(jnp.float32)).reshape(B_, n, C_W)
    return (y * scale.astype(jnp.float32)).astype(u.dtype)


def project(h, w_in):
    z = jnp.einsum('bnd,dc->bnc', h, w_in)
    return jnp.split(z, SPLITS, axis=-1)


def mla_query_latent(cq, ckv_raw, lw):
    B_, n, _ = cq.shape
    q = jnp.dot(rmsnorm(cq, lw['q_norm_g']), lw['w_uq']).reshape(B_, n, N_HEADS_B, QK_NOPE + QK_ROPE)
    ckv = rmsnorm(ckv_raw, lw['kv_norm_g'])
    return q[..., :QK_NOPE], q[..., QK_NOPE:], ckv


def mla_expand(ckv, lw):
    B_, n, _ = ckv.shape
    kv = jnp.dot(ckv, lw['w_ukv']).reshape(B_, n, N_HEADS_B, QK_NOPE + V_HD)
    return kv[..., :QK_NOPE], kv[..., QK_NOPE:]


def merge(oa, ob, oc, gates, lw):
    ga, gb, gc = jnp.split(gates, N_BRANCH, axis=-1)
    y = (jax.nn.sigmoid(ga) * jnp.dot(oa, lw['wpa'])
         + jax.nn.sigmoid(gb) * jnp.dot(ob, lw['wpb'])
         + jax.nn.sigmoid(gc) * jnp.dot(oc, lw['wpc']))
    return jnp.dot(y, lw['w_out'])


def mixer_context(h, lw):
    B_, n, _ = h.shape
    qa, ka, va, cq, ckv_raw, kr, u, gates = project(h, lw['w_in'])
    qa = qa.reshape(B_, n, N_HEADS_A, HD_A)
    ka = ka.reshape(B_, n, N_KV_A, HD_A)
    va = va.reshape(B_, n, N_KV_A, HD_A)
    oa = ctx_attn(qa, ka, va, lw['sink'])
    q_nope, q_rope, ckv = mla_query_latent(cq, ckv_raw, lw)
    k_nope, vb = mla_expand(ckv, lw)
    ob = mla_attn(q_nope, q_rope, k_nope, kr, vb)
    oc = pool_mixer(u, lw['pool_w'], lw['pool_scale'])
    return merge(oa, ob, oc, gates, lw), (ka, va, ckv, kr)


def mixer_latent(h, lw, ctx_k, ctx_v, ctx_ckv, ctx_kr, ang_a, ang_b):
    B_, n, _ = h.shape
    qa, ka, va, cq, ckv_raw, kr, u, gates = project(h, lw['w_in'])
    qa = rope2d(qa.reshape(B_, n, N_HEADS_A, HD_A), ang_a)
    ka = rope2d(ka.reshape(B_, n, N_KV_A, HD_A), ang_a)
    va = va.reshape(B_, n, N_KV_A, HD_A)
    oa = local_ctx_attn(qa, ka, va, ctx_k, ctx_v, lw['sink'])
    q_nope, q_rope, ckv = mla_query_latent(cq, ckv_raw, lw)
    q_rope = rope2d(q_rope, ang_b)
    kr = rope2d(kr[:, :, None, :], ang_b)[:, :, 0, :]
    k_nope, vb = mla_expand(jnp.concatenate([ckv, ctx_ckv.astype(ckv.dtype)], axis=1), lw)
    kr_all = jnp.concatenate([kr, ctx_kr.astype(kr.dtype)], axis=1)
    ob = mla_attn(q_nope, q_rope, k_nope, kr_all, vb)
    oc = pool_mixer(u, lw['pool_w'], lw['pool_scale'])
    return merge(oa, ob, oc, gates, lw)


def swiglu(h, wg, wu, wd):
    return jnp.dot(jax.nn.silu(jnp.dot(h, wg)) * jnp.dot(h, wu), wd)


def moe_ffn(h, router_w, wg, wu, wd):
    B_, n, D_ = h.shape
    t = h.reshape(B_ * n, D_)
    logits = jnp.dot(t, router_w, preferred_element_type=jnp.float32)
    top_v, top_i = lax.top_k(logits, TOP_K)
    top_w = jax.nn.softmax(top_v, axis=-1)
    gate = jnp.sum(jax.nn.one_hot(top_i, N_EXPERTS, dtype=jnp.float32) * top_w[..., None], axis=1)
    y = jnp.zeros(t.shape, jnp.float32)
    for e in range(N_EXPERTS):
        y = y + gate[:, e:e + 1] * swiglu(t, wg[e], wu[e], wd[e]).astype(jnp.float32)
    return y.astype(h.dtype).reshape(B_, n, D_)


def setup_inputs(seed: int = 0) -> dict:
    key = jax.random.key(seed)
    ks = iter(jax.random.split(key, 40))

    def nrm(shape, s):
        return jax.random.normal(next(ks), shape, jnp.float32) * s

    n_dense = (DEPTH + 1) // 2
    n_moe = DEPTH // 2
    return {
        'x_prompt': nrm((BATCH, SEQ, D_MODEL), 1.0),
        'x_sample': nrm((DEC_BATCH, DEC_SEQ, D_MODEL), 1.0),
        'cache_attn_k': nrm((DEC_BATCH, DEPTH, PAST_LEN, N_KV_A, HD_A), 1.0),
        'cache_attn_v': nrm((DEC_BATCH, DEPTH, PAST_LEN, N_KV_A, HD_A), 1.0),
        'cache_mla_ckv': nrm((DEC_BATCH, DEPTH, PAST_LEN, KV_LORA), 1.0),
        'cache_mla_krope': nrm((DEC_BATCH, DEPTH, PAST_LEN, QK_ROPE), 1.0),
        'c': nrm((DEC_BATCH, D_MODEL), 1.0),
        'c_ctx': nrm((D_MODEL,), 1.0),
        'ln1_g': 1.0 + nrm((DEPTH, D_MODEL), 0.02),
        'ln2_g': 1.0 + nrm((DEPTH, D_MODEL), 0.02),
        'w_ada': nrm((DEPTH, D_MODEL, 6 * D_MODEL), D_MODEL ** -0.5),
        'b_ada': nrm((DEPTH, 6 * D_MODEL), 0.01),
        'w_in': nrm((DEPTH, D_MODEL, IN_COLS), D_MODEL ** -0.5),
        'attn_sink': nrm((DEPTH, N_HEADS_A), 0.5),
        'mla_q_norm_g': 1.0 + nrm((DEPTH, Q_LORA), 0.02),
        'w_uq': nrm((DEPTH, Q_LORA, N_HEADS_B * (QK_NOPE + QK_ROPE)), Q_LORA ** -0.5),
        'mla_kv_norm_g': 1.0 + nrm((DEPTH, KV_LORA), 0.02),
        'w_ukv': nrm((DEPTH, KV_LORA, N_HEADS_B * (QK_NOPE + V_HD)), KV_LORA ** -0.5),
        'pool_w': nrm((DEPTH, POOL_GROUPS, POOL_GW, POOL_GW), POOL_GW ** -0.5),
        'pool_scale': 1.0 + nrm((DEPTH, C_W), 0.1),
        'w_branch_a': nrm((DEPTH, A_W, D_MODEL), A_W ** -0.5),
        'w_branch_b': nrm((DEPTH, B_W, D_MODEL), B_W ** -0.5),
        'w_branch_c': nrm((DEPTH, C_W, D_MODEL), C_W ** -0.5),
        'w_out': nrm((DEPTH, D_MODEL, D_MODEL), D_MODEL ** -0.5),
        'ffn_w_gate': nrm((n_dense, D_MODEL, D_FF), D_MODEL ** -0.5),
        'ffn_w_up': nrm((n_dense, D_MODEL, D_FF), D_MODEL ** -0.5),
        'ffn_w_down': nrm((n_dense, D_FF, D_MODEL), D_FF ** -0.5),
        'router_w': nrm((n_moe, D_MODEL, N_EXPERTS), D_MODEL ** -0.5),
        'moe_w_gate': nrm((n_moe, N_EXPERTS, D_MODEL, D_FF_E), D_MODEL ** -0.5),
        'moe_w_up': nrm((n_moe, N_EXPERTS, D_MODEL, D_FF_E), D_MODEL ** -0.5),
        'moe_w_down': nrm((n_moe, N_EXPERTS, D_FF_E, D_MODEL), D_FF_E ** -0.5),
        'final_g': 1.0 + nrm((D_MODEL,), 0.02),
    }


def reference(x_prompt, x_sample, cache_attn_k, cache_attn_v, cache_mla_ckv, cache_mla_krope,
              c, c_ctx, ln1_g, ln2_g, w_ada, b_ada, w_in, attn_sink, mla_q_norm_g, w_uq,
              mla_kv_norm_g, w_ukv, pool_w, pool_scale, w_branch_a, w_branch_b, w_branch_c,
              w_out, ffn_w_gate, ffn_w_up, ffn_w_down, router_w, moe_w_gate, moe_w_up,
              moe_w_down, final_g):
    n_lat = x_sample.shape[1]
    rows = n_lat // GRID_W
    ang_a = axial_angles(rows, HD_A)
    ang_b = axial_angles(rows, QK_ROPE)
    xp, xs = x_prompt, x_sample
    st_k, st_v, st_ckv, st_kr = [], [], [], []
    for l in range(DEPTH):
        lw = {'w_in': w_in[l], 'sink': attn_sink[l], 'q_norm_g': mla_q_norm_g[l], 'w_uq': w_uq[l],
              'kv_norm_g': mla_kv_norm_g[l], 'w_ukv': w_ukv[l], 'pool_w': pool_w[l],
              'pool_scale': pool_scale[l], 'wpa': w_branch_a[l], 'wpb': w_branch_b[l],
              'wpc': w_branch_c[l], 'w_out': w_out[l]}
        sh1p, sc1p, g1p, sh2p, sc2p, g2p = adaln(c_ctx[None, :], w_ada[l], b_ada[l])
        sh1s, sc1s, g1s, sh2s, sc2s, g2s = adaln(c, w_ada[l], b_ada[l])
        hp = modulate(rmsnorm(xp, ln1_g[l]), sh1p, sc1p)
        op, (ka, va, ckv, kr) = mixer_context(hp, lw)
        xp = xp + g1p * op
        hs = modulate(rmsnorm(xs, ln1_g[l]), sh1s, sc1s)
        osm = mixer_latent(hs, lw, cache_attn_k[:, l], cache_attn_v[:, l],
                           cache_mla_ckv[:, l], cache_mla_krope[:, l], ang_a, ang_b)
        xs = xs + g1s * osm
        st_k.append(ka)
        st_v.append(va)
        st_ckv.append(ckv)
        st_kr.append(kr)
        hp = modulate(rmsnorm(xp, ln2_g[l]), sh2p, sc2p)
        hs = modulate(rmsnorm(xs, ln2_g[l]), sh2s, sc2s)
        i = l // 2
        if l % 2 == 0:
            fp = swiglu(hp, ffn_w_gate[i], ffn_w_up[i], ffn_w_down[i])
            fs = swiglu(hs, ffn_w_gate[i], ffn_w_up[i], ffn_w_down[i])
        else:
            fp = moe_ffn(hp, router_w[i], moe_w_gate[i], moe_w_up[i], moe_w_down[i])
            fs = moe_ffn(hs, router_w[i], moe_w_gate[i], moe_w_up[i], moe_w_down[i])
        xp = xp + g2p * fp
        xs = xs + g2s * fs
    y_prompt = rmsnorm(xp, final_g)
    y_sample = rmsnorm(xs, final_g)
    state_attn_k = jnp.stack(st_k, axis=1)
    state_attn_v = jnp.stack(st_v, axis=1)
    state_mla_ckv = jnp.stack(st_ckv, axis=1)
    state_mla_krope = jnp.stack(st_kr, axis=1)
    return (y_prompt, y_sample, state_attn_k, state_attn_v, state_mla_ckv, state_mla_krope)
```

```python
import functools
import math

import jax
import jax.numpy as jnp
from jax import lax
from jax.experimental import pallas as pl
from jax.experimental.pallas import tpu as pltpu

BF16 = jnp.bfloat16
F32 = jnp.float32

D_MODEL = 2048
BATCH = 16
SEQ = 256
DEPTH = 2
DEC_BATCH = 8
DEC_SEQ = 2048
PAST_LEN = 512
GRID_W = 64
BLK = 128
EPS = 1e-6
ROPE_BASE = 10000.0
NEG_INF = -1e30
N_HEADS_A = 16
N_KV_A = 2
HD_A = 64
WINDOW = 128
A_W = N_HEADS_A * HD_A
KV_W = N_KV_A * HD_A
N_HEADS_B = 8
Q_LORA = 512
KV_LORA = 256
QK_NOPE = 128
QK_ROPE = 64
V_HD = 128
B_W = N_HEADS_B * V_HD
POOL_WINDOWS = (2, 4, 8, 16)
POOL_GROUPS = 4
POOL_GW = 256
C_W = POOL_GROUPS * POOL_GW
OFF_K = A_W
OFF_V = OFF_K + KV_W
OFF_CQ = OFF_V + KV_W
OFF_CKV = OFF_CQ + Q_LORA
OFF_KR = OFF_CKV + KV_LORA
OFF_POOL = OFF_KR + QK_ROPE
OFF_GATE = OFF_POOL + C_W
IN_COLS = OFF_GATE + 3 * D_MODEL
D_FF = 5632
N_EXPERTS = 8
TOP_K = 2

TP = BATCH * SEQ
TS = DEC_BATCH * DEC_SEQ
T = TP + TS
N_COND = 16
MLA_KEYS = DEC_SEQ + PAST_LEN
LANES = 128
POOL_HALO = 8

VMEM_LIMIT = 56 * 1024 * 1024

TM = 1024
TM_FFN = 512
TF = 512
NF = D_FF // TF
MOE_TILES = (TOP_K * T) // TM_FFN + N_EXPERTS
TC = 256
TQ_MLA = 256


def _cparams(sem):
    return pltpu.CompilerParams(dimension_semantics=sem, vmem_limit_bytes=VMEM_LIMIT)


def _cond_row(i, tm):
    n_p = TP // tm
    per_b = DEC_SEQ // tm
    return jnp.where(i < n_p, 0, (i - n_p) // per_b + 1)


def _dot(a, b):
    return jnp.dot(a, b, preferred_element_type=F32)


def _dot_nt(a, b):
    return lax.dot_general(a, b, (((1,), (1,)), ((), ())), preferred_element_type=F32)


def _mm_kernel(x_ref, w_ref, o_ref):
    o_ref[...] = _dot(x_ref[...].astype(BF16), w_ref[...].astype(BF16)).astype(o_ref.dtype)


def _matmul(x, w, out_dtype, tm, tn, name):
    m, k = x.shape
    n = w.shape[1]
    return pl.pallas_call(
        _mm_kernel,
        grid=(m // tm, n // tn),
        in_specs=[pl.BlockSpec((tm, k), lambda i, j: (i, 0)),
                  pl.BlockSpec((k, tn), lambda i, j: (0, j))],
        out_specs=pl.BlockSpec((tm, tn), lambda i, j: (i, j)),
        out_shape=jax.ShapeDtypeStruct((m, n), out_dtype),
        compiler_params=_cparams(("parallel", "arbitrary")),
        name=name,
    )(x, w)


def _ada_kernel(c_ref, w_ref, b_ref, o_ref):
    c = c_ref[...]
    a = (c * jax.nn.sigmoid(c)).astype(BF16)
    o_ref[...] = _dot(a, w_ref[...].astype(BF16)) + b_ref[...]


def _adaln(cond, w, b):
    n = w.shape[1]
    tn = 1024
    return pl.pallas_call(
        _ada_kernel,
        grid=(n // tn,),
        in_specs=[pl.BlockSpec((N_COND, D_MODEL), lambda j: (0, 0)),
                  pl.BlockSpec((D_MODEL, tn), lambda j: (0, j)),
                  pl.BlockSpec((1, tn), lambda j: (0, j))],
        out_specs=pl.BlockSpec((N_COND, tn), lambda j: (0, j)),
        out_shape=jax.ShapeDtypeStruct((N_COND, n), F32),
        compiler_params=_cparams(("arbitrary",)),
        name="adaln",
    )(cond, w, b.reshape(1, n))


def _norm_mod_kernel(x_ref, g_ref, sh_ref, sc_ref, o_ref):
    x = x_ref[...]
    y = x * lax.rsqrt(jnp.mean(x * x, axis=-1, keepdims=True) + EPS) * g_ref[...]
    o_ref[...] = (y * (1 + sc_ref[...]) + sh_ref[...]).astype(o_ref.dtype)


def _norm_mod(x, g, mods, k_shift, k_scale, out_dtype, name):
    tm = 512
    return pl.pallas_call(
        _norm_mod_kernel,
        grid=(T // tm,),
        in_specs=[pl.BlockSpec((tm, D_MODEL), lambda i: (i, 0)),
                  pl.BlockSpec((1, D_MODEL), lambda i: (0, 0)),
                  pl.BlockSpec((None, 1, D_MODEL), lambda i: (_cond_row(i, tm), 0, k_shift)),
                  pl.BlockSpec((None, 1, D_MODEL), lambda i: (_cond_row(i, tm), 0, k_scale))],
        out_specs=pl.BlockSpec((tm, D_MODEL), lambda i: (i, 0)),
        out_shape=jax.ShapeDtypeStruct((T, D_MODEL), out_dtype),
        compiler_params=_cparams(("parallel",)),
        name=name,
    )(x, g.reshape(1, D_MODEL), mods, mods)


def _final_norm_kernel(x_ref, g_ref, o_ref):
    x = x_ref[...]
    o_ref[...] = x * lax.rsqrt(jnp.mean(x * x, axis=-1, keepdims=True) + EPS) * g_ref[...]


def _final_norm(x, g):
    tm = 512
    return pl.pallas_call(
        _final_norm_kernel,
        grid=(T // tm,),
        in_specs=[pl.BlockSpec((tm, D_MODEL), lambda i: (i, 0)),
                  pl.BlockSpec((1, D_MODEL), lambda i: (0, 0))],
        out_specs=pl.BlockSpec((tm, D_MODEL), lambda i: (i, 0)),
        out_shape=jax.ShapeDtypeStruct((T, D_MODEL), F32),
        compiler_params=_cparams(("parallel",)),
        name="final_norm",
    )(x, g.reshape(1, D_MODEL))


def _rope_tables(rows, dim):
    quarter = dim // 4
    inv = ROPE_BASE ** (-jnp.arange(quarter, dtype=F32) / quarter)
    r = jnp.repeat(jnp.arange(rows, dtype=F32), GRID_W)
    col = jnp.tile(jnp.arange(GRID_W, dtype=F32), rows)
    ar, ac = r[:, None] * inv, col[:, None] * inv
    cos = jnp.concatenate([jnp.cos(ar), jnp.cos(ar), jnp.cos(ac), jnp.cos(ac)], axis=-1)
    sin = jnp.concatenate([-jnp.sin(ar), jnp.sin(ar), -jnp.sin(ac), jnp.sin(ac)], axis=-1)
    reps = LANES // dim
    return jnp.tile(cos, (1, reps)), jnp.tile(sin, (1, reps))


def _rope_lanes(x, cos, sin, quarter):
    lane = lax.broadcasted_iota(jnp.int32, (x.shape[0], LANES), 1)
    first = (lane % (2 * quarter)) < quarter
    outs = []
    for c in range(x.shape[1] // LANES):
        xc = x[:, c * LANES:(c + 1) * LANES]
        partner = jnp.where(first, pltpu.roll(xc, LANES - quarter, 1), pltpu.roll(xc, quarter, 1))
        outs.append(xc * cos + partner * sin)
    return outs[0] if len(outs) == 1 else jnp.concatenate(outs, axis=1)


def _attn_a_kernel(*refs, local, tq, n_blocks):
    if local:
        (sink_ref, q_ref, kc_ref, vc_ref, kvm_ref, kv0_ref, kvp_ref,
         cq_ref, sq_ref, ckm_ref, skm_ref, ck0_ref, sk0_ref, ckp_ref, skp_ref, o_ref) = refs
    else:
        sink_ref, q_ref, kc_ref, vc_ref, o_ref = refs
    scale = HD_A ** -0.5
    quarter = HD_A // 4
    q = q_ref[...]
    kc = kc_ref[...].astype(BF16)
    vc = vc_ref[...].astype(BF16)
    if local:
        j = pl.program_id(1)
        q = _rope_lanes(q, cq_ref[...], sq_ref[...], quarter)
        kl = jnp.concatenate([
            _rope_lanes(kvm_ref[:, :KV_W], ckm_ref[...], skm_ref[...], quarter),
            _rope_lanes(kv0_ref[:, :KV_W], ck0_ref[...], sk0_ref[...], quarter),
            _rope_lanes(kvp_ref[:, :KV_W], ckp_ref[...], skp_ref[...], quarter)], axis=0).astype(BF16)
        vl = jnp.concatenate([kvm_ref[:, KV_W:], kv0_ref[:, KV_W:], kvp_ref[:, KV_W:]], axis=0).astype(BF16)
        qi = lax.broadcasted_iota(jnp.int32, (tq, 3 * BLK), 0)
        ki = lax.broadcasted_iota(jnp.int32, (tq, 3 * BLK), 1)
        k_lo = jnp.where(j == 0, BLK, 0)
        k_hi = jnp.where(j == n_blocks - 1, 2 * BLK, 3 * BLK)
        valid = (ki >= qi) & (ki <= qi + 2 * WINDOW) & (ki >= k_lo) & (ki < k_hi)
    q = q.astype(BF16)
    rep = N_HEADS_A // N_KV_A
    outs = []
    for h in range(N_HEADS_A):
        g = h // rep
        qh = q[:, h * HD_A:(h + 1) * HD_A]
        sk = sink_ref[h]
        s_ctx = _dot_nt(qh, kc[:, g * HD_A:(g + 1) * HD_A]) * scale
        m = jnp.maximum(jnp.max(s_ctx, axis=-1, keepdims=True), sk)
        if local:
            s_loc = _dot_nt(qh, kl[:, g * HD_A:(g + 1) * HD_A]) * scale
            s_loc = jnp.where(valid, s_loc, NEG_INF)
            m = jnp.maximum(m, jnp.max(s_loc, axis=-1, keepdims=True))
        p_ctx = jnp.exp(s_ctx - m)
        denom = jnp.sum(p_ctx, axis=-1, keepdims=True) + jnp.exp(sk - m)
        o = _dot(p_ctx.astype(BF16), vc[:, g * HD_A:(g + 1) * HD_A])
        if local:
            p_loc = jnp.exp(s_loc - m)
            denom = denom + jnp.sum(p_loc, axis=-1, keepdims=True)
            o = o + _dot(p_loc.astype(BF16), vl[:, g * HD_A:(g + 1) * HD_A])
        outs.append(o / denom)
    o_ref[...] = jnp.concatenate(outs, axis=1).astype(o_ref.dtype)


def _attn_a_prompt(qa, kv, sink):
    kern = functools.partial(_attn_a_kernel, local=False, tq=SEQ, n_blocks=1)
    return pl.pallas_call(
        kern,
        grid=(BATCH,),
        in_specs=[pl.BlockSpec(memory_space=pltpu.SMEM),
                  pl.BlockSpec((SEQ, A_W), lambda b: (b, 0)),
                  pl.BlockSpec((SEQ, KV_W), lambda b: (b, 0)),
                  pl.BlockSpec((SEQ, KV_W), lambda b: (b, 1))],
        out_specs=pl.BlockSpec((SEQ, A_W), lambda b: (b, 0)),
        out_shape=jax.ShapeDtypeStruct((TP, A_W), BF16),
        compiler_params=_cparams(("parallel",)),
        name="attn_a_prompt",
    )(sink, qa, kv, kv)


def _attn_a_sample(qa, kv, ck, cv, sink, cos, sin):
    nb = DEC_SEQ // BLK
    base = TP // BLK

    def row(b, j):
        return base + b * nb + j

    def tab_prev(b, j):
        return (jnp.maximum(j - 1, 0), 0)

    def tab_next(b, j):
        return (jnp.minimum(j + 1, nb - 1), 0)

    kv_spec = lambda f: pl.BlockSpec((BLK, 2 * KV_W), f)
    tab_spec = lambda f: pl.BlockSpec((BLK, LANES), f)
    kern = functools.partial(_attn_a_kernel, local=True, tq=BLK, n_blocks=nb)
    return pl.pallas_call(
        kern,
        grid=(DEC_BATCH, nb),
        in_specs=[pl.BlockSpec(memory_space=pltpu.SMEM),
                  pl.BlockSpec((BLK, A_W), lambda b, j: (row(b, j), 0)),
                  pl.BlockSpec((None, PAST_LEN, KV_W), lambda b, j: (b, 0, 0)),
                  pl.BlockSpec((None, PAST_LEN, KV_W), lambda b, j: (b, 0, 0)),
                  kv_spec(lambda b, j: (row(b, jnp.maximum(j - 1, 0)), 0)),
                  kv_spec(lambda b, j: (row(b, j), 0)),
                  kv_spec(lambda b, j: (row(b, jnp.minimum(j + 1, nb - 1)), 0)),
                  tab_spec(lambda b, j: (j, 0)), tab_spec(lambda b, j: (j, 0)),
                  tab_spec(tab_prev), tab_spec(tab_prev),
                  tab_spec(lambda b, j: (j, 0)), tab_spec(lambda b, j: (j, 0)),
                  tab_spec(tab_next), tab_spec(tab_next)],
        out_specs=pl.BlockSpec((BLK, A_W), lambda b, j: (b * nb + j, 0)),
        out_shape=jax.ShapeDtypeStruct((TS, A_W), BF16),
        compiler_params=_cparams(("parallel", "arbitrary")),
        name="attn_a_sample",
    )(sink, qa, ck, cv, kv, kv, kv, cos, sin, cos, sin, cos, sin, cos, sin)


def _mla_prep_kernel(z_ref, gq_ref, gkv_ref, wn_ref, wr_ref, qn_ref, qr_ref, ckv_ref, kr_ref):
    cq = z_ref[:, :Q_LORA]
    cqn = (cq * lax.rsqrt(jnp.mean(cq * cq, axis=-1, keepdims=True) + EPS) * gq_ref[...]).astype(BF16)
    qn_ref[...] = _dot(cqn, wn_ref[...]).astype(qn_ref.dtype)
    qr_ref[...] = _dot(cqn, wr_ref[...])
    c = z_ref[:, Q_LORA:Q_LORA + KV_LORA]
    ckv_ref[...] = c * lax.rsqrt(jnp.mean(c * c, axis=-1, keepdims=True) + EPS) * gkv_ref[...]
    kr_ref[...] = z_ref[:, Q_LORA + KV_LORA:]


def _mla_prep(z, gq, gkv, w_n, w_r):
    tm = TM
    zc = Q_LORA + KV_LORA + QK_ROPE
    full = lambda shape: pl.BlockSpec(shape, lambda i: (0, 0))
    rows = lambda c: pl.BlockSpec((tm, c), lambda i: (i, 0))
    return pl.pallas_call(
        _mla_prep_kernel,
        grid=(T // tm,),
        in_specs=[rows(zc), full((1, Q_LORA)), full((1, KV_LORA)),
                  full((Q_LORA, N_HEADS_B * QK_NOPE)), full((Q_LORA, N_HEADS_B * QK_ROPE))],
        out_specs=[rows(N_HEADS_B * QK_NOPE), rows(N_HEADS_B * QK_ROPE), rows(KV_LORA), rows(QK_ROPE)],
        out_shape=[jax.ShapeDtypeStruct((T, N_HEADS_B * QK_NOPE), BF16),
                   jax.ShapeDtypeStruct((T, N_HEADS_B * QK_ROPE), F32),
                   jax.ShapeDtypeStruct((T, KV_LORA), F32),
                   jax.ShapeDtypeStruct((T, QK_ROPE), F32)],
        compiler_params=_cparams(("parallel",)),
        name="mla_prep",
    )(z, gq.reshape(1, Q_LORA), gkv.reshape(1, KV_LORA), w_n, w_r)


def _rope_rows_kernel(x_ref, cos_ref, sin_ref, o_ref):
    x = x_ref[...]
    x2 = jnp.concatenate([x, x], axis=1)
    o_ref[...] = _rope_lanes(x2, cos_ref[...], sin_ref[...], QK_ROPE // 4)[:, :QK_ROPE].astype(o_ref.dtype)


def _rope_k(kr, cos, sin):
    tm = DEC_SEQ
    base = TP // tm
    return pl.pallas_call(
        _rope_rows_kernel,
        grid=(DEC_BATCH,),
        in_specs=[pl.BlockSpec((tm, QK_ROPE), lambda b: (base + b, 0)),
                  pl.BlockSpec((tm, LANES), lambda b: (0, 0)),
                  pl.BlockSpec((tm, LANES), lambda b: (0, 0))],
        out_specs=pl.BlockSpec((tm, QK_ROPE), lambda b: (b, 0)),
        out_shape=jax.ShapeDtypeStruct((TS, QK_ROPE), BF16),
        compiler_params=_cparams(("parallel",)),
        name="rope_k",
    )(kr, cos, sin)


def _mla_kernel(*refs, rope):
    if rope:
        qn_ref, qr_ref, kn_ref, v_ref, kr_ref, cos_ref, sin_ref, o_ref = refs
    else:
        qn_ref, qr_ref, kn_ref, v_ref, kr_ref, o_ref = refs
    scale = (QK_NOPE + QK_ROPE) ** -0.5
    qr = qr_ref[...]
    if rope:
        qr = _rope_lanes(qr, cos_ref[...], sin_ref[...], QK_ROPE // 4)
    qr = qr.astype(BF16)
    kr = kr_ref[...].astype(BF16)
    outs = []
    for h in range(N_HEADS_B):
        qn = qn_ref[:, h * QK_NOPE:(h + 1) * QK_NOPE]
        kn = kn_ref[:, h * QK_NOPE:(h + 1) * QK_NOPE]
        s = (_dot_nt(qn, kn) + _dot_nt(qr[:, h * QK_ROPE:(h + 1) * QK_ROPE], kr)) * scale
        p = jnp.exp(s - jnp.max(s, axis=-1, keepdims=True))
        denom = jnp.sum(p, axis=-1, keepdims=True)
        outs.append(_dot(p.astype(BF16), v_ref[:, h * V_HD:(h + 1) * V_HD]) / denom)
    o_ref[...] = jnp.concatenate(outs, axis=1).astype(o_ref.dtype)


def _mla_prompt(qn, qr, kv, kr):
    kern = functools.partial(_mla_kernel, rope=False)
    return pl.pallas_call(
        kern,
        grid=(BATCH,),
        in_specs=[pl.BlockSpec((SEQ, N_HEADS_B * QK_NOPE), lambda b: (b, 0)),
                  pl.BlockSpec((SEQ, N_HEADS_B * QK_ROPE), lambda b: (b, 0)),
                  pl.BlockSpec((SEQ, B_W), lambda b: (b, 0)),
                  pl.BlockSpec((SEQ, B_W), lambda b: (b, 1)),
                  pl.BlockSpec((SEQ, QK_ROPE), lambda b: (b, 0))],
        out_specs=pl.BlockSpec((SEQ, B_W), lambda b: (b, 0)),
        out_shape=jax.ShapeDtypeStruct((TP, B_W), BF16),
        compiler_params=_cparams(("parallel",)),
        name="mla_prompt",
    )(qn, qr, kv, kv, kr)


def _mla_sample(qn, qr, kv, kr_all, cos, sin):
    tq = TQ_MLA
    nq = DEC_SEQ // tq
    base = TP // tq
    kern = functools.partial(_mla_kernel, rope=True)
    return pl.pallas_call(
        kern,
        grid=(DEC_BATCH, nq),
        in_specs=[pl.BlockSpec((tq, N_HEADS_B * QK_NOPE), lambda b, i: (base + b * nq + i, 0)),
                  pl.BlockSpec((tq, N_HEADS_B * QK_ROPE), lambda b, i: (base + b * nq + i, 0)),
                  pl.BlockSpec((MLA_KEYS, B_W), lambda b, i: (b, 0)),
                  pl.BlockSpec((MLA_KEYS, B_W), lambda b, i: (b, 1)),
                  pl.BlockSpec((None, MLA_KEYS, QK_ROPE), lambda b, i: (b, 0, 0)),
                  pl.BlockSpec((tq, LANES), lambda b, i: (i, 0)),
                  pl.BlockSpec((tq, LANES), lambda b, i: (i, 0))],
        out_specs=pl.BlockSpec((tq, B_W), lambda b, i: (b * nq + i, 0)),
        out_shape=jax.ShapeDtypeStruct((TS, B_W), BF16),
        compiler_params=_cparams(("parallel", "arbitrary")),
        name="mla_sample",
    )(qn, qr, kv, kv, kr_all, cos, sin)


def _pool_kernel(u_ref, w_ref, s_ref, o_ref, pad_ref, *, n):
    chunk = min(n, 256)
    zeros = jnp.zeros((POOL_HALO, C_W), F32)
    pad_ref[0:POOL_HALO, :] = zeros
    pad_ref[POOL_HALO + n:POOL_HALO + n + POOL_HALO, :] = zeros
    pad_ref[POOL_HALO:POOL_HALO + n, :] = u_ref[...]
    for c in range(n // chunk):
        r0 = c * chunk
        t = lax.broadcasted_iota(jnp.int32, (chunk, 1), 0) + r0
        for g, win in enumerate(POOL_WINDOWS):
            left = win // 2
            right = win - left - 1
            cols = slice(g * POOL_GW, (g + 1) * POOL_GW)
            acc = pad_ref[POOL_HALO + r0 - left:POOL_HALO + r0 - left + chunk, cols]
            for k in range(-left + 1, right + 1):
                acc = acc + pad_ref[POOL_HALO + r0 + k:POOL_HALO + r0 + k + chunk, cols]
            cnt = (jnp.minimum(t + right, n - 1) + 1 - jnp.maximum(t - left, 0)).astype(F32)
            d = acc / cnt - pad_ref[POOL_HALO + r0:POOL_HALO + r0 + chunk, cols]
            y = _dot(d.astype(BF16), w_ref[g]) * s_ref[:, cols]
            o_ref[r0:r0 + chunk, cols] = y.astype(o_ref.dtype)


def _pool(u, w, scale, n, n_seq, row_base, name):
    kern = functools.partial(_pool_kernel, n=n)
    return pl.pallas_call(
        kern,
        grid=(n_seq,),
        in_specs=[pl.BlockSpec((n, C_W), lambda b: (row_base + b, 0)),
                  pl.BlockSpec((POOL_GROUPS, POOL_GW, POOL_GW), lambda b: (0, 0, 0)),
                  pl.BlockSpec((1, C_W), lambda b: (0, 0))],
        out_specs=pl.BlockSpec((n, C_W), lambda b: (b, 0)),
        out_shape=jax.ShapeDtypeStruct((n_seq * n, C_W), BF16),
        scratch_shapes=[pltpu.VMEM((n + 2 * POOL_HALO, C_W), F32)],
        compiler_params=_cparams(("parallel",)),
        name=name,
    )(u, w, scale.reshape(1, C_W))


def _merge_kernel(oa_ref, ob_ref, oc_ref, ga_ref, gb_ref, gc_ref, wa_ref, wb_ref, wc_ref, o_ref):
    y = jax.nn.sigmoid(ga_ref[...].astype(F32)) * _dot(oa_ref[...], wa_ref[...])
    y = y + jax.nn.sigmoid(gb_ref[...].astype(F32)) * _dot(ob_ref[...], wb_ref[...])
    y = y + jax.nn.sigmoid(gc_ref[...].astype(F32)) * _dot(oc_ref[...], wc_ref[...])
    o_ref[...] = y.astype(o_ref.dtype)


def _merge(oa, ob, oc, gates, wa, wb, wc):
    tm, tn = TM, 1024
    nn = D_MODEL // tn
    rows = pl.BlockSpec((tm, A_W), lambda i, j: (i, 0))
    gate = lambda k: pl.BlockSpec((tm, tn), lambda i, j: (i, k * nn + j))
    wspec = pl.BlockSpec((A_W, tn), lambda i, j: (0, j))
    return pl.pallas_call(
        _merge_kernel,
        grid=(T // tm, nn),
        in_specs=[rows, rows, rows, gate(0), gate(1), gate(2), wspec, wspec, wspec],
        out_specs=pl.BlockSpec((tm, tn), lambda i, j: (i, j)),
        out_shape=jax.ShapeDtypeStruct((T, D_MODEL), BF16),
        compiler_params=_cparams(("parallel", "arbitrary")),
        name="merge",
    )(oa, ob, oc, gates, gates, gates, wa, wb, wc)


def _mm_res_kernel(y_ref, w_ref, x_ref, g_ref, o_ref):
    o_ref[...] = x_ref[...] + g_ref[...] * _dot(y_ref[...], w_ref[...])


def _matmul_residual(y, w, x, mods, k_gate):
    tm, tn = TM, 1024
    return pl.pallas_call(
        _mm_res_kernel,
        grid=(T // tm, D_MODEL // tn),
        in_specs=[pl.BlockSpec((tm, D_MODEL), lambda i, j: (i, 0)),
                  pl.BlockSpec((D_MODEL, tn), lambda i, j: (0, j)),
                  pl.BlockSpec((tm, tn), lambda i, j: (i, j)),
                  pl.BlockSpec((None, 1, tn), lambda i, j: (_cond_row(i, tm), 0, k_gate * (D_MODEL // tn) + j))],
        out_specs=pl.BlockSpec((tm, tn), lambda i, j: (i, j)),
        out_shape=jax.ShapeDtypeStruct((T, D_MODEL), F32),
        compiler_params=_cparams(("parallel", "arbitrary")),
        name="out_proj",
    )(y, w, x, mods)


def _swiglu_step(x, wg_ref, wu_ref, wd_ref):
    g = _dot(x, wg_ref[...])
    u = _dot(x, wu_ref[...])
    h = (g * jax.nn.sigmoid(g) * u).astype(BF16)
    return _dot(h, wd_ref[...])


def _ffn_dense_kernel(x_ref, wg_ref, wu_ref, wd_ref, xres_ref, gate_ref, o_ref, acc_ref):
    j = pl.program_id(1)
    c = _swiglu_step(x_ref[...], wg_ref, wu_ref, wd_ref)

    @pl.when(j == 0)
    def _():
        acc_ref[...] = c

    @pl.when(j > 0)
    def _():
        acc_ref[...] += c

    @pl.when(j == NF - 1)
    def _():
        o_ref[...] = xres_ref[...] + gate_ref[...] * acc_ref[...]


def _ffn_dense(h, wg, wu, wd, x, mods, k_gate):
    tm = TM_FFN
    return pl.pallas_call(
        _ffn_dense_kernel,
        grid=(T // tm, NF),
        in_specs=[pl.BlockSpec((tm, D_MODEL), lambda i, j: (i, 0)),
                  pl.BlockSpec((D_MODEL, TF), lambda i, j: (0, j)),
                  pl.BlockSpec((D_MODEL, TF), lambda i, j: (0, j)),
                  pl.BlockSpec((TF, D_MODEL), lambda i, j: (j, 0)),
                  pl.BlockSpec((tm, D_MODEL), lambda i, j: (i, 0)),
                  pl.BlockSpec((None, 1, D_MODEL), lambda i, j: (_cond_row(i, tm), 0, k_gate))],
        out_specs=pl.BlockSpec((tm, D_MODEL), lambda i, j: (i, 0)),
        out_shape=jax.ShapeDtypeStruct((T, D_MODEL), F32),
        scratch_shapes=[pltpu.VMEM((tm, D_MODEL), F32)],
        compiler_params=_cparams(("parallel", "arbitrary")),
        name="ffn_dense",
    )(h, wg, wu, wd, x, mods)


def _router_kernel(x_ref, whi_ref, wlo_ref, idx_ref, wt_ref):
    x = x_ref[...]
    hi = x.astype(BF16)
    lo = (x - hi.astype(F32)).astype(BF16)
    logits = _dot(hi, whi_ref[...]) + _dot(lo, whi_ref[...]) + _dot(hi, wlo_ref[...])
    lane = lax.broadcasted_iota(jnp.int32, logits.shape, 1)
    lg = jnp.where(lane < N_EXPERTS, logits, -jnp.inf)
    v1 = jnp.max(lg, axis=-1, keepdims=True)
    i1 = jnp.min(jnp.where(lg == v1, lane, LANES), axis=-1, keepdims=True)
    lg2 = jnp.where(lane == i1, -jnp.inf, lg)
    v2 = jnp.max(lg2, axis=-1, keepdims=True)
    i2 = jnp.min(jnp.where(lg2 == v2, lane, LANES), axis=-1, keepdims=True)
    e2 = jnp.exp(v2 - v1)
    tot = 1.0 + e2
    idx_ref[...] = jnp.where(lane == 0, i1, jnp.where(lane == 1, i2, 0))
    wt_ref[...] = jnp.where(lane == 0, 1.0 / tot, jnp.where(lane == 1, e2 / tot, 0.0))


def _router(h, w_hi, w_lo):
    tm = 512
    return pl.pallas_call(
        _router_kernel,
        grid=(T // tm,),
        in_specs=[pl.BlockSpec((tm, D_MODEL), lambda i: (i, 0)),
                  pl.BlockSpec((D_MODEL, LANES), lambda i: (0, 0)),
                  pl.BlockSpec((D_MODEL, LANES), lambda i: (0, 0))],
        out_specs=[pl.BlockSpec((tm, LANES), lambda i: (i, 0)),
                   pl.BlockSpec((tm, LANES), lambda i: (i, 0))],
        out_shape=[jax.ShapeDtypeStruct((T, LANES), jnp.int32),
                   jax.ShapeDtypeStruct((T, LANES), F32)],
        compiler_params=_cparams(("parallel",)),
        name="router",
    )(h, w_hi, w_lo)


def _dispatch(idx):
    e = idx[:, :TOP_K].reshape(-1)
    onehot = (e[:, None] == jnp.arange(N_EXPERTS, dtype=jnp.int32)[None, :]).astype(jnp.int32)
    csum = jnp.cumsum(onehot, axis=0)
    rank = jnp.take_along_axis(csum, e[:, None], axis=1)[:, 0] - 1
    cnt = csum[-1]
    padded = ((cnt + TM_FFN - 1) // TM_FFN) * TM_FFN
    g_end = jnp.cumsum(padded)
    slot = (g_end - padded)[e] + rank
    tok = jnp.arange(TOP_K * T, dtype=jnp.int32) // TOP_K
    row_ids = jnp.zeros((MOE_TILES * TM_FFN,), jnp.int32).at[slot].set(tok)
    end_tiles = g_end // TM_FFN
    n_used = end_tiles[-1]
    tiles = jnp.arange(MOE_TILES, dtype=jnp.int32)
    tile_e = jnp.minimum(jnp.searchsorted(end_tiles, tiles, side="right").astype(jnp.int32), N_EXPERTS - 1)
    last_e = tile_e[jnp.maximum(n_used - 1, 0)]
    tile_e = jnp.where(tiles < n_used, tile_e, last_e)
    return (row_ids.reshape(MOE_TILES, TM_FFN), tile_e, n_used.reshape(1).astype(jnp.int32),
            slot.reshape(T, TOP_K).astype(jnp.int32))


def _row_copy(src_hbm, src_row, dst_vmem, dst_row, sem):
    return pltpu.make_async_copy(src_hbm.at[pl.ds(src_row, 1), :], dst_vmem.at[pl.ds(dst_row, 1), :], sem)


def _ffn_moe_kernel(te_ref, nu_ref, ids_hbm, x_hbm, wg_ref, wu_ref, wd_ref, o_ref,
                    ids_smem, xg_ref, xb_ref, sem_ids, sem_rows):
    i = pl.program_id(0)
    j = pl.program_id(1)
    used = i < nu_ref[0]

    @pl.when(used & (j == 0))
    def _():
        ids_copy = pltpu.make_async_copy(ids_hbm.at[i], ids_smem, sem_ids)
        ids_copy.start()
        ids_copy.wait()

        def start(r, carry):
            _row_copy(x_hbm, ids_smem[r], xg_ref, r, sem_rows).start()
            return carry

        lax.fori_loop(0, TM_FFN, start, 0)

        def wait(r, carry):
            _row_copy(x_hbm, 0, xg_ref, r, sem_rows).wait()
            return carry

        lax.fori_loop(0, TM_FFN, wait, 0)
        xb_ref[...] = xg_ref[...].astype(BF16)

    @pl.when(used)
    def _():
        c = _swiglu_step(xb_ref[...], wg_ref, wu_ref, wd_ref)

        @pl.when(j == 0)
        def _():
            o_ref[...] = c

        @pl.when(j > 0)
        def _():
            o_ref[...] += c

    @pl.when(jnp.logical_not(used) & (j == 0))
    def _():
        o_ref[...] = jnp.zeros_like(o_ref)


def _ffn_moe(h, row_ids, tile_e, n_used, wg, wu, wd):
    def hid(i, j, te, nu):
        return jnp.where(i < nu[0], j, NF - 1)

    grid_spec = pltpu.PrefetchScalarGridSpec(
        num_scalar_prefetch=2,
        grid=(MOE_TILES, NF),
        in_specs=[pl.BlockSpec(memory_space=pl.ANY),
                  pl.BlockSpec(memory_space=pl.ANY),
                  pl.BlockSpec((None, D_MODEL, TF), lambda i, j, te, nu: (te[i], 0, hid(i, j, te, nu))),
                  pl.BlockSpec((None, D_MODEL, TF), lambda i, j, te, nu: (te[i], 0, hid(i, j, te, nu))),
                  pl.BlockSpec((None, TF, D_MODEL), lambda i, j, te, nu: (te[i], hid(i, j, te, nu), 0))],
        out_specs=pl.BlockSpec((TM_FFN, D_MODEL), lambda i, j, te, nu: (i, 0)),
        scratch_shapes=[pltpu.SMEM((TM_FFN,), jnp.int32),
                        pltpu.VMEM((TM_FFN, D_MODEL), F32),
                        pltpu.VMEM((TM_FFN, D_MODEL), BF16),
                        pltpu.SemaphoreType.DMA,
                        pltpu.SemaphoreType.DMA],
    )
    return pl.pallas_call(
        _ffn_moe_kernel,
        grid_spec=grid_spec,
        out_shape=jax.ShapeDtypeStruct((MOE_TILES * TM_FFN, D_MODEL), F32),
        compiler_params=_cparams(("arbitrary", "arbitrary")),
        name="ffn_moe",
    )(tile_e, n_used, row_ids, h, wg, wu, wd)


def _combine_kernel(slot_hbm, ys_hbm, wt_ref, x_ref, gate_ref, o_ref, slot_smem, y0_ref, y1_ref, sem_ids, sem_rows):
    i = pl.program_id(0)
    ids_copy = pltpu.make_async_copy(slot_hbm.at[i], slot_smem, sem_ids)
    ids_copy.start()
    ids_copy.wait()

    def start(r, carry):
        _row_copy(ys_hbm, slot_smem[TOP_K * r], y0_ref, r, sem_rows).start()
        _row_copy(ys_hbm, slot_smem[TOP_K * r + 1], y1_ref, r, sem_rows).start()
        return carry

    lax.fori_loop(0, TC, start, 0)

    def wait(r, carry):
        _row_copy(ys_hbm, 0, y0_ref, r, sem_rows).wait()
        _row_copy(ys_hbm, 0, y1_ref, r, sem_rows).wait()
        return carry

    lax.fori_loop(0, TC, wait, 0)
    wt = wt_ref[...]
    y = wt[:, 0:1] * y0_ref[...] + wt[:, 1:2] * y1_ref[...]
    o_ref[...] = x_ref[...] + gate_ref[...] * y


def _combine(slot, ys, wt, x, mods, k_gate):
    return pl.pallas_call(
        _combine_kernel,
        grid=(T // TC,),
        in_specs=[pl.BlockSpec(memory_space=pl.ANY),
                  pl.BlockSpec(memory_space=pl.ANY),
                  pl.BlockSpec((TC, LANES), lambda i: (i, 0)),
                  pl.BlockSpec((TC, D_MODEL), lambda i: (i, 0)),
                  pl.BlockSpec((None, 1, D_MODEL), lambda i: (_cond_row(i, TC), 0, k_gate))],
        out_specs=pl.BlockSpec((TC, D_MODEL), lambda i: (i, 0)),
        out_shape=jax.ShapeDtypeStruct((T, D_MODEL), F32),
        scratch_shapes=[pltpu.SMEM((TOP_K * TC,), jnp.int32),
                        pltpu.VMEM((TC, D_MODEL), F32),
                        pltpu.VMEM((TC, D_MODEL), F32),
                        pltpu.SemaphoreType.DMA,
                        pltpu.SemaphoreType.DMA],
        compiler_params=_cparams(("arbitrary",)),
        name="moe_combine",
    )(slot.reshape(T // TC, TOP_K * TC), ys, wt, x, mods)


def kernel(x_prompt, x_sample, cache_attn_k, cache_attn_v, cache_mla_ckv, cache_mla_krope, c, c_ctx, ln1_g, ln2_g, w_ada, b_ada, w_in, attn_sink, mla_q_norm_g, w_uq, mla_kv_norm_g, w_ukv, pool_w, pool_scale, w_branch_a, w_branch_b, w_branch_c, w_out, ffn_w_gate, ffn_w_up, ffn_w_down, router_w, moe_w_gate, moe_w_up, moe_w_down, final_g):
    x = jnp.concatenate([x_prompt.reshape(TP, D_MODEL), x_sample.reshape(TS, D_MODEL)], axis=0)
    cond = jnp.concatenate([c_ctx[None, :], c, jnp.zeros((N_COND - 1 - DEC_BATCH, D_MODEL), F32)], axis=0)
    rows = DEC_SEQ // GRID_W
    cos_a, sin_a = _rope_tables(rows, HD_A)
    cos_b, sin_b = _rope_tables(rows, QK_ROPE)
    st_k, st_v, st_ckv, st_kr = [], [], [], []
    for l in range(DEPTH):
        mods = _adaln(cond, w_ada[l], b_ada[l]).reshape(N_COND, 1, 6 * D_MODEL)
        h1 = _norm_mod(x, ln1_g[l], mods, 0, 1, BF16, "norm_mod1")
        wl = w_in[l]
        qa = _matmul(h1, wl[:, :OFF_K].astype(BF16), F32, TM, A_W, "proj_q")
        kv = _matmul(h1, wl[:, OFF_K:OFF_CQ].astype(BF16), F32, TM, 2 * KV_W, "proj_kv")
        zb = _matmul(h1, wl[:, OFF_CQ:OFF_POOL].astype(BF16), F32, TM, OFF_POOL - OFF_CQ, "proj_mla")
        u = _matmul(h1, wl[:, OFF_POOL:OFF_GATE].astype(BF16), F32, TM, C_W, "proj_pool")
        gates = _matmul(h1, wl[:, OFF_GATE:].astype(BF16), BF16, TM, 1024, "proj_gates")
        sink = attn_sink[l]
        ck = cache_attn_k[:, l].reshape(DEC_BATCH, PAST_LEN, KV_W)
        cv = cache_attn_v[:, l].reshape(DEC_BATCH, PAST_LEN, KV_W)
        oa = jnp.concatenate([_attn_a_prompt(qa, kv, sink),
                              _attn_a_sample(qa, kv, ck, cv, sink, cos_a, sin_a)], axis=0)
        wq = w_uq[l].reshape(Q_LORA, N_HEADS_B, QK_NOPE + QK_ROPE)
        wq_n = wq[:, :, :QK_NOPE].reshape(Q_LORA, N_HEADS_B * QK_NOPE).astype(BF16)
        wq_r = wq[:, :, QK_NOPE:].reshape(Q_LORA, N_HEADS_B * QK_ROPE).astype(BF16)
        wkv = w_ukv[l].reshape(KV_LORA, N_HEADS_B, QK_NOPE + V_HD)
        wkv = jnp.concatenate([wkv[:, :, :QK_NOPE].reshape(KV_LORA, B_W),
                               wkv[:, :, QK_NOPE:].reshape(KV_LORA, B_W)], axis=1).astype(BF16)
        qn, qr, ckv, kr = _mla_prep(zb, mla_q_norm_g[l], mla_kv_norm_g[l], wq_n, wq_r)
        kv_p = _matmul(ckv[:TP], wkv, BF16, TM, 2 * B_W, "mla_kv_prompt")
        ckv_all = jnp.concatenate([ckv[TP:].reshape(DEC_BATCH, DEC_SEQ, KV_LORA), cache_mla_ckv[:, l]], axis=1)
        kv_s = _matmul(ckv_all.reshape(DEC_BATCH * MLA_KEYS, KV_LORA), wkv, BF16, TM, 2 * B_W, "mla_kv_sample")
        kr_all = jnp.concatenate([_rope_k(kr, cos_b, sin_b).reshape(DEC_BATCH, DEC_SEQ, QK_ROPE),
                                  cache_mla_krope[:, l].astype(BF16)], axis=1)
        ob = jnp.concatenate([_mla_prompt(qn, qr, kv_p, kr),
                              _mla_sample(qn, qr, kv_s, kr_all, cos_b, sin_b)], axis=0)
        pw = pool_w[l].astype(BF16)
        oc = jnp.concatenate([_pool(u, pw, pool_scale[l], SEQ, BATCH, 0, "pool_prompt"),
                              _pool(u, pw, pool_scale[l], DEC_SEQ, DEC_BATCH, TP // DEC_SEQ, "pool_sample")], axis=0)
        y = _merge(oa, ob, oc, gates, w_branch_a[l].astype(BF16), w_branch_b[l].astype(BF16),
                   w_branch_c[l].astype(BF16))
        x = _matmul_residual(y, w_out[l].astype(BF16), x, mods, 2)
        st_k.append(kv[:TP, :KV_W].reshape(BATCH, SEQ, N_KV_A, HD_A))
        st_v.append(kv[:TP, KV_W:].reshape(BATCH, SEQ, N_KV_A, HD_A))
        st_ckv.append(ckv[:TP].reshape(BATCH, SEQ, KV_LORA))
        st_kr.append(kr[:TP].reshape(BATCH, SEQ, QK_ROPE))
        i = l // 2
        if l % 2 == 0:
            h2 = _norm_mod(x, ln2_g[l], mods, 3, 4, BF16, "norm_mod2")
            x = _ffn_dense(h2, ffn_w_gate[i].astype(BF16), ffn_w_up[i].astype(BF16),
                           ffn_w_down[i].astype(BF16), x, mods, 5)
        else:
            h2 = _norm_mod(x, ln2_g[l], mods, 3, 4, F32, "norm_mod2_f32")
            rw = jnp.pad(router_w[i], ((0, 0), (0, LANES - N_EXPERTS)))
            rw_hi = rw.astype(BF16)
            rw_lo = (rw - rw_hi.astype(F32)).astype(BF16)
            idx, wt = _router(h2, rw_hi, rw_lo)
            row_ids, tile_e, n_used, slot = _dispatch(idx)
            ys = _ffn_moe(h2, row_ids, tile_e, n_used, moe_w_gate[i].astype(BF16),
                          moe_w_up[i].astype(BF16), moe_w_down[i].astype(BF16))
            x = _combine(slot, ys, wt, x, mods, 5)
    y = _final_norm(x, final_g)
    return (y[:TP].reshape(BATCH, SEQ, D_MODEL), y[TP:].reshape(DEC_BATCH, DEC_SEQ, D_MODEL),
            jnp.stack(st_k, axis=1), jnp.stack(st_v, axis=1), jnp.stack(st_ckv, axis=1), jnp.stack(st_kr, axis=1))
```

```python
import functools
import math

import jax
import jax.numpy as jnp
from jax import lax
from jax.experimental import pallas as pl
from jax.experimental.pallas import tpu as pltpu

BF16 = jnp.bfloat16
F32 = jnp.float32

D_MODEL = 2048
BATCH = 16
SEQ = 256
DEPTH = 2
DEC_BATCH = 8
DEC_SEQ = 2048
PAST_LEN = 512
GRID_W = 64
BLK = 128
EPS = 1e-6
ROPE_BASE = 10000.0
NEG_INF = -1e30
N_HEADS_A = 16
N_KV_A = 2
HD_A = 64
WINDOW = 128
A_W = N_HEADS_A * HD_A
KV_W = N_KV_A * HD_A
N_HEADS_B = 8
Q_LORA = 512
KV_LORA = 256
QK_NOPE = 128
QK_ROPE = 64
V_HD = 128
B_W = N_HEADS_B * V_HD
POOL_WINDOWS = (2, 4, 8, 16)
POOL_GROUPS = 4
POOL_GW = 256
C_W = POOL_GROUPS * POOL_GW
OFF_K = A_W
OFF_V = OFF_K + KV_W
OFF_CQ = OFF_V + KV_W
OFF_CKV = OFF_CQ + Q_LORA
OFF_KR = OFF_CKV + KV_LORA
OFF_POOL = OFF_KR + QK_ROPE
OFF_GATE = OFF_POOL + C_W
IN_COLS = OFF_GATE + 3 * D_MODEL
D_FF = 5632
N_EXPERTS = 8
TOP_K = 2

TP = BATCH * SEQ
TS = DEC_BATCH * DEC_SEQ
T = TP + TS
N_COND = 16
MLA_KEYS = DEC_SEQ + PAST_LEN
LANES = 128
POOL_HALO = 8

VMEM_LIMIT = 56 * 1024 * 1024

TM = 1024
TM_FFN = 1024
TF = 512
NF = D_FF // TF
MOE_TILES = (TOP_K * T) // TM_FFN + N_EXPERTS
TR = 512
ZR = 256
TC = 256
TQ_MLA = 256


def _cparams(sem):
    return pltpu.CompilerParams(dimension_semantics=sem, vmem_limit_bytes=VMEM_LIMIT)


def _cond_row(i, tm):
    n_p = TP // tm
    per_b = DEC_SEQ // tm
    return jnp.where(i < n_p, 0, (i - n_p) // per_b + 1)


def _dot(a, b):
    return jnp.dot(a, b, preferred_element_type=F32)


def _dot_nt(a, b):
    return lax.dot_general(a, b, (((1,), (1,)), ((), ())), preferred_element_type=F32)


def _mm_kernel(x_ref, w_ref, o_ref):
    o_ref[...] = _dot(x_ref[...].astype(BF16), w_ref[...].astype(BF16)).astype(o_ref.dtype)


def _matmul(x, w, out_dtype, tm, tn, name):
    m, k = x.shape
    n = w.shape[1]
    return pl.pallas_call(
        _mm_kernel,
        grid=(m // tm, n // tn),
        in_specs=[pl.BlockSpec((tm, k), lambda i, j: (i, 0)),
                  pl.BlockSpec((k, tn), lambda i, j: (0, j))],
        out_specs=pl.BlockSpec((tm, tn), lambda i, j: (i, j)),
        out_shape=jax.ShapeDtypeStruct((m, n), out_dtype),
        compiler_params=_cparams(("parallel", "arbitrary")),
        name=name,
    )(x, w)


def _ada_kernel(c_ref, w_ref, b_ref, o_ref):
    c = c_ref[...]
    a = (c * jax.nn.sigmoid(c)).astype(BF16)
    o_ref[...] = _dot(a, w_ref[...].astype(BF16)) + b_ref[...]


def _adaln(cond, w, b):
    n = w.shape[1]
    tn = 1024
    return pl.pallas_call(
        _ada_kernel,
        grid=(n // tn,),
        in_specs=[pl.BlockSpec((N_COND, D_MODEL), lambda j: (0, 0)),
                  pl.BlockSpec((D_MODEL, tn), lambda j: (0, j)),
                  pl.BlockSpec((1, tn), lambda j: (0, j))],
        out_specs=pl.BlockSpec((N_COND, tn), lambda j: (0, j)),
        out_shape=jax.ShapeDtypeStruct((N_COND, n), F32),
        compiler_params=_cparams(("arbitrary",)),
        name="adaln",
    )(cond, w, b.reshape(1, n))


def _norm_mod_kernel(x_ref, g_ref, sh_ref, sc_ref, o_ref):
    x = x_ref[...]
    y = x * lax.rsqrt(jnp.mean(x * x, axis=-1, keepdims=True) + EPS) * g_ref[...]
    o_ref[...] = (y * (1 + sc_ref[...]) + sh_ref[...]).astype(o_ref.dtype)


def _norm_mod(x, g, mods, k_shift, k_scale, out_dtype, name):
    tm = 512
    return pl.pallas_call(
        _norm_mod_kernel,
        grid=(T // tm,),
        in_specs=[pl.BlockSpec((tm, D_MODEL), lambda i: (i, 0)),
                  pl.BlockSpec((1, D_MODEL), lambda i: (0, 0)),
                  pl.BlockSpec((None, 1, D_MODEL), lambda i: (_cond_row(i, tm), 0, k_shift)),
                  pl.BlockSpec((None, 1, D_MODEL), lambda i: (_cond_row(i, tm), 0, k_scale))],
        out_specs=pl.BlockSpec((tm, D_MODEL), lambda i: (i, 0)),
        out_shape=jax.ShapeDtypeStruct((T, D_MODEL), out_dtype),
        compiler_params=_cparams(("parallel",)),
        name=name,
    )(x, g.reshape(1, D_MODEL), mods, mods)


def _rms(x, g):
    return x * lax.rsqrt(jnp.mean(x * x, axis=-1, keepdims=True) + EPS) * g


def _resid_norm_kernel(x_ref, f_ref, gate_ref, g_ref, sh_ref, sc_ref, x2_ref, h_ref):
    x2 = x_ref[...] + gate_ref[...] * f_ref[...]
    x2_ref[...] = x2
    h_ref[...] = (_rms(x2, g_ref[...]) * (1 + sc_ref[...]) + sh_ref[...]).astype(h_ref.dtype)


def _resid_norm(x, f, mods, k_gate, g, mods_next, k_shift, k_scale):
    tm = 512
    rows = pl.BlockSpec((tm, D_MODEL), lambda i: (i, 0))
    mod = lambda k: pl.BlockSpec((None, 1, D_MODEL), lambda i: (_cond_row(i, tm), 0, k))
    return pl.pallas_call(
        _resid_norm_kernel,
        grid=(T // tm,),
        in_specs=[rows, rows, mod(k_gate), pl.BlockSpec((1, D_MODEL), lambda i: (0, 0)), mod(k_shift), mod(k_scale)],
        out_specs=[rows, rows],
        out_shape=[jax.ShapeDtypeStruct((T, D_MODEL), F32), jax.ShapeDtypeStruct((T, D_MODEL), BF16)],
        compiler_params=_cparams(("parallel",)),
        name="resid_norm",
    )(x, f, mods, g.reshape(1, D_MODEL), mods_next, mods_next)


def _rope_tables(rows, dim):
    quarter = dim // 4
    inv = ROPE_BASE ** (-jnp.arange(quarter, dtype=F32) / quarter)
    r = jnp.repeat(jnp.arange(rows, dtype=F32), GRID_W)
    col = jnp.tile(jnp.arange(GRID_W, dtype=F32), rows)
    ar, ac = r[:, None] * inv, col[:, None] * inv
    cos = jnp.concatenate([jnp.cos(ar), jnp.cos(ar), jnp.cos(ac), jnp.cos(ac)], axis=-1)
    sin = jnp.concatenate([-jnp.sin(ar), jnp.sin(ar), -jnp.sin(ac), jnp.sin(ac)], axis=-1)
    reps = LANES // dim
    return jnp.tile(cos, (1, reps)), jnp.tile(sin, (1, reps))


def _rope_lanes(x, cos, sin, quarter):
    lane = lax.broadcasted_iota(jnp.int32, (x.shape[0], LANES), 1)
    first = (lane % (2 * quarter)) < quarter
    outs = []
    for c in range(x.shape[1] // LANES):
        xc = x[:, c * LANES:(c + 1) * LANES]
        partner = jnp.where(first, pltpu.roll(xc, LANES - quarter, 1), pltpu.roll(xc, quarter, 1))
        outs.append(xc * cos + partner * sin)
    return outs[0] if len(outs) == 1 else jnp.concatenate(outs, axis=1)


def _attn_a_kernel(*refs, local, tq, n_blocks):
    if local:
        (sink_ref, q_ref, kc_ref, vc_ref, kvm_ref, kv0_ref, kvp_ref,
         cq_ref, sq_ref, ckm_ref, skm_ref, ck0_ref, sk0_ref, ckp_ref, skp_ref, o_ref) = refs
    else:
        sink_ref, q_ref, kc_ref, vc_ref, o_ref = refs
    scale = HD_A ** -0.5
    quarter = HD_A // 4
    q = q_ref[...]
    kc = kc_ref[...].astype(BF16)
    vc = vc_ref[...].astype(BF16)
    if local:
        j = pl.program_id(1)
        q = _rope_lanes(q, cq_ref[...], sq_ref[...], quarter)
        kl = jnp.concatenate([
            _rope_lanes(kvm_ref[:, :KV_W], ckm_ref[...], skm_ref[...], quarter),
            _rope_lanes(kv0_ref[:, :KV_W], ck0_ref[...], sk0_ref[...], quarter),
            _rope_lanes(kvp_ref[:, :KV_W], ckp_ref[...], skp_ref[...], quarter)], axis=0).astype(BF16)
        vl = jnp.concatenate([kvm_ref[:, KV_W:], kv0_ref[:, KV_W:], kvp_ref[:, KV_W:]], axis=0).astype(BF16)
        qi = lax.broadcasted_iota(jnp.int32, (tq, 3 * BLK), 0)
        ki = lax.broadcasted_iota(jnp.int32, (tq, 3 * BLK), 1)
        k_lo = jnp.where(j == 0, BLK, 0)
        k_hi = jnp.where(j == n_blocks - 1, 2 * BLK, 3 * BLK)
        valid = (ki >= qi) & (ki <= qi + 2 * WINDOW) & (ki >= k_lo) & (ki < k_hi)
    q = q.astype(BF16)
    rep = N_HEADS_A // N_KV_A
    outs = []
    for h in range(N_HEADS_A):
        g = h // rep
        qh = q[:, h * HD_A:(h + 1) * HD_A]
        sk = sink_ref[h]
        s_ctx = _dot_nt(qh, kc[:, g * HD_A:(g + 1) * HD_A]) * scale
        m = jnp.maximum(jnp.max(s_ctx, axis=-1, keepdims=True), sk)
        if local:
            s_loc = _dot_nt(qh, kl[:, g * HD_A:(g + 1) * HD_A]) * scale
            s_loc = jnp.where(valid, s_loc, NEG_INF)
            m = jnp.maximum(m, jnp.max(s_loc, axis=-1, keepdims=True))
        p_ctx = jnp.exp(s_ctx - m)
        denom = jnp.sum(p_ctx, axis=-1, keepdims=True) + jnp.exp(sk - m)
        o = _dot(p_ctx.astype(BF16), vc[:, g * HD_A:(g + 1) * HD_A])
        if local:
            p_loc = jnp.exp(s_loc - m)
            denom = denom + jnp.sum(p_loc, axis=-1, keepdims=True)
            o = o + _dot(p_loc.astype(BF16), vl[:, g * HD_A:(g + 1) * HD_A])
        outs.append(o / denom)
    o_ref[...] = jnp.concatenate(outs, axis=1).astype(o_ref.dtype)


def _attn_a_prompt(qa, kv, sink):
    kern = functools.partial(_attn_a_kernel, local=False, tq=SEQ, n_blocks=1)
    return pl.pallas_call(
        kern,
        grid=(BATCH,),
        in_specs=[pl.BlockSpec(memory_space=pltpu.SMEM),
                  pl.BlockSpec((SEQ, A_W), lambda b: (b, 0)),
                  pl.BlockSpec((SEQ, KV_W), lambda b: (b, 0)),
                  pl.BlockSpec((SEQ, KV_W), lambda b: (b, 1))],
        out_specs=pl.BlockSpec((SEQ, A_W), lambda b: (b, 0)),
        out_shape=jax.ShapeDtypeStruct((TP, A_W), BF16),
        compiler_params=_cparams(("parallel",)),
        name="attn_a_prompt",
    )(sink, qa, kv, kv)


def _attn_a_sample(qa, kv, ck, cv, sink, cos, sin):
    nb = DEC_SEQ // BLK
    base = TP // BLK

    def row(b, j):
        return base + b * nb + j

    def tab_prev(b, j):
        return (jnp.maximum(j - 1, 0), 0)

    def tab_next(b, j):
        return (jnp.minimum(j + 1, nb - 1), 0)

    kv_spec = lambda f: pl.BlockSpec((BLK, 2 * KV_W), f)
    tab_spec = lambda f: pl.BlockSpec((BLK, LANES), f)
    kern = functools.partial(_attn_a_kernel, local=True, tq=BLK, n_blocks=nb)
    return pl.pallas_call(
        kern,
        grid=(DEC_BATCH, nb),
        in_specs=[pl.BlockSpec(memory_space=pltpu.SMEM),
                  pl.BlockSpec((BLK, A_W), lambda b, j: (row(b, j), 0)),
                  pl.BlockSpec((None, PAST_LEN, KV_W), lambda b, j: (b, 0, 0)),
                  pl.BlockSpec((None, PAST_LEN, KV_W), lambda b, j: (b, 0, 0)),
                  kv_spec(lambda b, j: (row(b, jnp.maximum(j - 1, 0)), 0)),
                  kv_spec(lambda b, j: (row(b, j), 0)),
                  kv_spec(lambda b, j: (row(b, jnp.minimum(j + 1, nb - 1)), 0)),
                  tab_spec(lambda b, j: (j, 0)), tab_spec(lambda b, j: (j, 0)),
                  tab_spec(tab_prev), tab_spec(tab_prev),
                  tab_spec(lambda b, j: (j, 0)), tab_spec(lambda b, j: (j, 0)),
                  tab_spec(tab_next), tab_spec(tab_next)],
        out_specs=pl.BlockSpec((BLK, A_W), lambda b, j: (b * nb + j, 0)),
        out_shape=jax.ShapeDtypeStruct((TS, A_W), BF16),
        compiler_params=_cparams(("parallel", "arbitrary")),
        name="attn_a_sample",
    )(sink, qa, ck, cv, kv, kv, kv, cos, sin, cos, sin, cos, sin, cos, sin)


def _mla_prep_kernel(z_ref, gq_ref, gkv_ref, wn_ref, wr_ref, qn_ref, qr_ref, ckv_ref, kr_ref):
    cq = z_ref[:, :Q_LORA]
    cqn = (cq * lax.rsqrt(jnp.mean(cq * cq, axis=-1, keepdims=True) + EPS) * gq_ref[...]).astype(BF16)
    qn_ref[...] = _dot(cqn, wn_ref[...]).astype(qn_ref.dtype)
    qr_ref[...] = _dot(cqn, wr_ref[...])
    c = z_ref[:, Q_LORA:Q_LORA + KV_LORA]
    ckv_ref[...] = c * lax.rsqrt(jnp.mean(c * c, axis=-1, keepdims=True) + EPS) * gkv_ref[...]
    kr_ref[...] = z_ref[:, Q_LORA + KV_LORA:]


def _mla_prep(z, gq, gkv, w_n, w_r):
    tm = TM
    zc = Q_LORA + KV_LORA + QK_ROPE
    full = lambda shape: pl.BlockSpec(shape, lambda i: (0, 0))
    rows = lambda c: pl.BlockSpec((tm, c), lambda i: (i, 0))
    return pl.pallas_call(
        _mla_prep_kernel,
        grid=(T // tm,),
        in_specs=[rows(zc), full((1, Q_LORA)), full((1, KV_LORA)),
                  full((Q_LORA, N_HEADS_B * QK_NOPE)), full((Q_LORA, N_HEADS_B * QK_ROPE))],
        out_specs=[rows(N_HEADS_B * QK_NOPE), rows(N_HEADS_B * QK_ROPE), rows(KV_LORA), rows(QK_ROPE)],
        out_shape=[jax.ShapeDtypeStruct((T, N_HEADS_B * QK_NOPE), BF16),
                   jax.ShapeDtypeStruct((T, N_HEADS_B * QK_ROPE), F32),
                   jax.ShapeDtypeStruct((T, KV_LORA), F32),
                   jax.ShapeDtypeStruct((T, QK_ROPE), F32)],
        compiler_params=_cparams(("parallel",)),
        name="mla_prep",
    )(z, gq.reshape(1, Q_LORA), gkv.reshape(1, KV_LORA), w_n, w_r)


def _rope_rows_kernel(x_ref, cos_ref, sin_ref, o_ref):
    x = x_ref[...]
    x2 = jnp.concatenate([x, x], axis=1)
    o_ref[...] = _rope_lanes(x2, cos_ref[...], sin_ref[...], QK_ROPE // 4)[:, :QK_ROPE].astype(o_ref.dtype)


def _rope_k(kr, cos, sin):
    tm = DEC_SEQ
    base = TP // tm
    return pl.pallas_call(
        _rope_rows_kernel,
        grid=(DEC_BATCH,),
        in_specs=[pl.BlockSpec((tm, QK_ROPE), lambda b: (base + b, 0)),
                  pl.BlockSpec((tm, LANES), lambda b: (0, 0)),
                  pl.BlockSpec((tm, LANES), lambda b: (0, 0))],
        out_specs=pl.BlockSpec((tm, QK_ROPE), lambda b: (b, 0)),
        out_shape=jax.ShapeDtypeStruct((TS, QK_ROPE), BF16),
        compiler_params=_cparams(("parallel",)),
        name="rope_k",
    )(kr, cos, sin)


def _mla_kernel(*refs, rope):
    if rope:
        qn_ref, qr_ref, kn_ref, v_ref, kr_ref, cos_ref, sin_ref, o_ref = refs
    else:
        qn_ref, qr_ref, kn_ref, v_ref, kr_ref, o_ref = refs
    scale = (QK_NOPE + QK_ROPE) ** -0.5
    qr = qr_ref[...]
    if rope:
        qr = _rope_lanes(qr, cos_ref[...], sin_ref[...], QK_ROPE // 4)
    qr = qr.astype(BF16)
    kr = kr_ref[...].astype(BF16)
    outs = []
    for h in range(N_HEADS_B):
        qn = qn_ref[:, h * QK_NOPE:(h + 1) * QK_NOPE]
        kn = kn_ref[:, h * QK_NOPE:(h + 1) * QK_NOPE]
        s = (_dot_nt(qn, kn) + _dot_nt(qr[:, h * QK_ROPE:(h + 1) * QK_ROPE], kr)) * scale
        p = jnp.exp(s - jnp.max(s, axis=-1, keepdims=True))
        denom = jnp.sum(p, axis=-1, keepdims=True)
        outs.append(_dot(p.astype(BF16), v_ref[:, h * V_HD:(h + 1) * V_HD]) / denom)
    o_ref[...] = jnp.concatenate(outs, axis=1).astype(o_ref.dtype)


def _mla_prompt(qn, qr, kv, kr):
    kern = functools.partial(_mla_kernel, rope=False)
    return pl.pallas_call(
        kern,
        grid=(BATCH,),
        in_specs=[pl.BlockSpec((SEQ, N_HEADS_B * QK_NOPE), lambda b: (b, 0)),
                  pl.BlockSpec((SEQ, N_HEADS_B * QK_ROPE), lambda b: (b, 0)),
                  pl.BlockSpec((SEQ, B_W), lambda b: (b, 0)),
                  pl.BlockSpec((SEQ, B_W), lambda b: (b, 1)),
                  pl.BlockSpec((SEQ, QK_ROPE), lambda b: (b, 0))],
        out_specs=pl.BlockSpec((SEQ, B_W), lambda b: (b, 0)),
        out_shape=jax.ShapeDtypeStruct((TP, B_W), BF16),
        compiler_params=_cparams(("parallel",)),
        name="mla_prompt",
    )(qn, qr, kv, kv, kr)


def _mla_sample(qn, qr, kv, kr_all, cos, sin):
    tq = TQ_MLA
    nq = DEC_SEQ // tq
    base = TP // tq
    kern = functools.partial(_mla_kernel, rope=True)
    return pl.pallas_call(
        kern,
        grid=(DEC_BATCH, nq),
        in_specs=[pl.BlockSpec((tq, N_HEADS_B * QK_NOPE), lambda b, i: (base + b * nq + i, 0)),
                  pl.BlockSpec((tq, N_HEADS_B * QK_ROPE), lambda b, i: (base + b * nq + i, 0)),
                  pl.BlockSpec((MLA_KEYS, B_W), lambda b, i: (b, 0)),
                  pl.BlockSpec((MLA_KEYS, B_W), lambda b, i: (b, 1)),
                  pl.BlockSpec((None, MLA_KEYS, QK_ROPE), lambda b, i: (b, 0, 0)),
                  pl.BlockSpec((tq, LANES), lambda b, i: (i, 0)),
                  pl.BlockSpec((tq, LANES), lambda b, i: (i, 0))],
        out_specs=pl.BlockSpec((tq, B_W), lambda b, i: (b * nq + i, 0)),
        out_shape=jax.ShapeDtypeStruct((TS, B_W), BF16),
        compiler_params=_cparams(("parallel", "arbitrary")),
        name="mla_sample",
    )(qn, qr, kv, kv, kr_all, cos, sin)


def _pool_kernel(u_ref, w_ref, s_ref, o_ref, pad_ref, *, n):
    chunk = min(n, 256)
    zeros = jnp.zeros((POOL_HALO, C_W), F32)
    pad_ref[0:POOL_HALO, :] = zeros
    pad_ref[POOL_HALO + n:POOL_HALO + n + POOL_HALO, :] = zeros
    pad_ref[POOL_HALO:POOL_HALO + n, :] = u_ref[...]
    for c in range(n // chunk):
        r0 = c * chunk
        t = lax.broadcasted_iota(jnp.int32, (chunk, 1), 0) + r0
        for g, win in enumerate(POOL_WINDOWS):
            left = win // 2
            right = win - left - 1
            cols = slice(g * POOL_GW, (g + 1) * POOL_GW)
            acc = pad_ref[POOL_HALO + r0 - left:POOL_HALO + r0 - left + chunk, cols]
            for k in range(-left + 1, right + 1):
                acc = acc + pad_ref[POOL_HALO + r0 + k:POOL_HALO + r0 + k + chunk, cols]
            cnt = (jnp.minimum(t + right, n - 1) + 1 - jnp.maximum(t - left, 0)).astype(F32)
            d = acc / cnt - pad_ref[POOL_HALO + r0:POOL_HALO + r0 + chunk, cols]
            y = _dot(d.astype(BF16), w_ref[g]) * s_ref[:, cols]
            o_ref[r0:r0 + chunk, cols] = y.astype(o_ref.dtype)


def _pool(u, w, scale, n, n_seq, row_base, name):
    kern = functools.partial(_pool_kernel, n=n)
    return pl.pallas_call(
        kern,
        grid=(n_seq,),
        in_specs=[pl.BlockSpec((n, C_W), lambda b: (row_base + b, 0)),
                  pl.BlockSpec((POOL_GROUPS, POOL_GW, POOL_GW), lambda b: (0, 0, 0)),
                  pl.BlockSpec((1, C_W), lambda b: (0, 0))],
        out_specs=pl.BlockSpec((n, C_W), lambda b: (b, 0)),
        out_shape=jax.ShapeDtypeStruct((n_seq * n, C_W), BF16),
        scratch_shapes=[pltpu.VMEM((n + 2 * POOL_HALO, C_W), F32)],
        compiler_params=_cparams(("parallel",)),
        name=name,
    )(u, w, scale.reshape(1, C_W))


def _merge_kernel(oa_ref, ob_ref, oc_ref, ga_ref, gb_ref, gc_ref, wa_ref, wb_ref, wc_ref, o_ref):
    y = jax.nn.sigmoid(ga_ref[...].astype(F32)) * _dot(oa_ref[...], wa_ref[...])
    y = y + jax.nn.sigmoid(gb_ref[...].astype(F32)) * _dot(ob_ref[...], wb_ref[...])
    y = y + jax.nn.sigmoid(gc_ref[...].astype(F32)) * _dot(oc_ref[...], wc_ref[...])
    o_ref[...] = y.astype(o_ref.dtype)


def _merge(oa, ob, oc, gates, wa, wb, wc):
    tm, tn = TM, 1024
    nn = D_MODEL // tn
    rows = pl.BlockSpec((tm, A_W), lambda i, j: (i, 0))
    gate = lambda k: pl.BlockSpec((tm, tn), lambda i, j: (i, k * nn + j))
    wspec = pl.BlockSpec((A_W, tn), lambda i, j: (0, j))
    return pl.pallas_call(
        _merge_kernel,
        grid=(T // tm, nn),
        in_specs=[rows, rows, rows, gate(0), gate(1), gate(2), wspec, wspec, wspec],
        out_specs=pl.BlockSpec((tm, tn), lambda i, j: (i, j)),
        out_shape=jax.ShapeDtypeStruct((T, D_MODEL), BF16),
        compiler_params=_cparams(("parallel", "arbitrary")),
        name="merge",
    )(oa, ob, oc, gates, gates, gates, wa, wb, wc)


def _out_proj_kernel(y_ref, w_ref, x_ref, gate_ref, g_ref, sh_ref, sc_ref, x1_ref, *h_ref):
    x1 = x_ref[...] + gate_ref[...] * _dot(y_ref[...], w_ref[...])
    x1_ref[...] = x1
    if h_ref:
        h_ref[0][...] = (_rms(x1, g_ref[...]) * (1 + sc_ref[...]) + sh_ref[...]).astype(h_ref[0].dtype)


def _out_proj(y, w, x, mods, g, emit_h):
    tm = 512
    rows = pl.BlockSpec((tm, D_MODEL), lambda i: (i, 0))
    mod = lambda k: pl.BlockSpec((None, 1, D_MODEL), lambda i: (_cond_row(i, tm), 0, k))
    n_out = 2 if emit_h else 1
    return pl.pallas_call(
        _out_proj_kernel,
        grid=(T // tm,),
        in_specs=[rows, pl.BlockSpec((D_MODEL, D_MODEL), lambda i: (0, 0)), rows, mod(2),
                  pl.BlockSpec((1, D_MODEL), lambda i: (0, 0)), mod(3), mod(4)],
        out_specs=[rows, rows][:n_out],
        out_shape=[jax.ShapeDtypeStruct((T, D_MODEL), F32), jax.ShapeDtypeStruct((T, D_MODEL), BF16)][:n_out],
        compiler_params=_cparams(("parallel",)),
        name="out_proj",
    )(y, w, x, mods, g.reshape(1, D_MODEL), mods, mods)


def _swiglu_step(x, wg_ref, wu_ref, wd_ref):
    g = _dot(x, wg_ref[...])
    u = _dot(x, wu_ref[...])
    h = (g * jax.nn.sigmoid(g) * u).astype(BF16)
    return _dot(h, wd_ref[...])


def _accumulate(o_ref, c, j):
    @pl.when(j == 0)
    def _():
        o_ref[...] = c

    @pl.when(j > 0)
    def _():
        o_ref[...] += c


def _ffn_dense_kernel(x_ref, wg_ref, wu_ref, wd_ref, o_ref):
    _accumulate(o_ref, _swiglu_step(x_ref[...], wg_ref, wu_ref, wd_ref), pl.program_id(1))


def _ffn_dense(h, wg, wu, wd):
    tm = TM_FFN
    return pl.pallas_call(
        _ffn_dense_kernel,
        grid=(T // tm, NF),
        in_specs=[pl.BlockSpec((tm, D_MODEL), lambda i, j: (i, 0)),
                  pl.BlockSpec((D_MODEL, TF), lambda i, j: (0, j)),
                  pl.BlockSpec((D_MODEL, TF), lambda i, j: (0, j)),
                  pl.BlockSpec((TF, D_MODEL), lambda i, j: (j, 0))],
        out_specs=pl.BlockSpec((tm, D_MODEL), lambda i, j: (i, 0)),
        out_shape=jax.ShapeDtypeStruct((T, D_MODEL), F32),
        compiler_params=_cparams(("parallel", "arbitrary")),
        name="ffn_dense",
    )(h, wg, wu, wd)


def _norm_router_kernel(x_ref, g_ref, sh_ref, sc_ref, whi_ref, wlo_ref, h_ref, meta_ref, wt_ref, cnt_ref, carry_ref):
    i = pl.program_id(0)

    @pl.when(i == 0)
    def _():
        carry_ref[...] = jnp.zeros_like(carry_ref)

    h = _rms(x_ref[...], g_ref[...]) * (1 + sc_ref[...]) + sh_ref[...]
    h_ref[...] = h
    hi = h.astype(BF16)
    lo = (h - hi.astype(F32)).astype(BF16)
    logits = _dot(hi, whi_ref[...]) + _dot(lo, whi_ref[...]) + _dot(hi, wlo_ref[...])
    lane = lax.broadcasted_iota(jnp.int32, logits.shape, 1)
    lg = jnp.where(lane < N_EXPERTS, logits, -jnp.inf)
    v1 = jnp.max(lg, axis=-1, keepdims=True)
    i1 = jnp.min(jnp.where(lg == v1, lane, LANES), axis=-1, keepdims=True)
    lg2 = jnp.where(lane == i1, -jnp.inf, lg)
    v2 = jnp.max(lg2, axis=-1, keepdims=True)
    i2 = jnp.min(jnp.where(lg2 == v2, lane, LANES), axis=-1, keepdims=True)
    e2 = jnp.exp(v2 - v1)
    tot = 1.0 + e2
    wt_ref[...] = jnp.where(lane == 0, 1.0 / tot, jnp.where(lane == 1, e2 / tot, 0.0))
    oh1 = (lane == i1).astype(F32)
    oh2 = (lane == i2).astype(F32)
    r = lax.broadcasted_iota(jnp.int32, (TR, TR), 0)
    c = lax.broadcasted_iota(jnp.int32, (TR, TR), 1)
    earlier = (r > c).astype(BF16)
    base = carry_ref[0:1, :]
    c1 = jnp.sum(oh1, axis=0, keepdims=True)
    c2 = jnp.sum(oh2, axis=0, keepdims=True)
    r1 = jnp.sum(oh1 * (base + _dot(earlier, oh1.astype(BF16))), axis=-1, keepdims=True)
    r2 = jnp.sum(oh2 * (base + c1 + _dot(earlier, oh2.astype(BF16))), axis=-1, keepdims=True)
    total = jnp.broadcast_to(base + c1 + c2, carry_ref.shape)
    carry_ref[...] = total
    cnt_ref[...] = total
    meta_ref[...] = jnp.where(lane == 0, i1, jnp.where(lane == 1, i2, jnp.where(
        lane == 2, r1.astype(jnp.int32), jnp.where(lane == 3, r2.astype(jnp.int32), 0))))


def _norm_router(x, g, mods, w_hi, w_lo):
    rows = lambda c: pl.BlockSpec((TR, c), lambda i: (i, 0))
    full = lambda shape: pl.BlockSpec(shape, lambda i: (0, 0))
    mod = lambda k: pl.BlockSpec((None, 1, D_MODEL), lambda i: (_cond_row(i, TR), 0, k))
    return pl.pallas_call(
        _norm_router_kernel,
        grid=(T // TR,),
        in_specs=[rows(D_MODEL), full((1, D_MODEL)), mod(3), mod(4), full((D_MODEL, LANES)), full((D_MODEL, LANES))],
        out_specs=[rows(D_MODEL), rows(LANES), rows(LANES), full((8, LANES))],
        out_shape=[jax.ShapeDtypeStruct((T, D_MODEL), F32),
                   jax.ShapeDtypeStruct((T, LANES), jnp.int32),
                   jax.ShapeDtypeStruct((T, LANES), F32),
                   jax.ShapeDtypeStruct((8, LANES), F32)],
        scratch_shapes=[pltpu.VMEM((8, LANES), F32)],
        compiler_params=_cparams(("arbitrary",)),
        name="norm_router",
    )(x, g.reshape(1, D_MODEL), mods, mods, w_hi, w_lo)


def _dispatch(meta, counts):
    experts = jnp.arange(N_EXPERTS, dtype=jnp.int32)
    cnt = counts[0, :N_EXPERTS].astype(jnp.int32)
    padded = ((cnt + TM_FFN - 1) // TM_FFN) * TM_FFN
    g_end = jnp.sum(jnp.where(experts[None, :] <= experts[:, None], padded[None, :], 0), axis=1)
    g_start = g_end - padded
    e = meta[:, :TOP_K]
    start_of = jnp.sum(jnp.where(e[:, :, None] == experts[None, None, :], g_start[None, None, :], 0), axis=-1)
    slot = start_of + meta[:, TOP_K:2 * TOP_K]
    end_tiles = g_end // TM_FFN
    n_used = end_tiles[N_EXPERTS - 1]
    tiles = jnp.arange(MOE_TILES, dtype=jnp.int32)
    owner = lambda t: jnp.minimum(jnp.sum((end_tiles[None, :] <= t[:, None]).astype(jnp.int32), axis=1), N_EXPERTS - 1)
    tile_e = owner(jnp.minimum(tiles, n_used - 1))
    return slot, g_end, padded, tile_e, n_used.reshape(1)


def _row_copy(src, src_row, dst, dst_row, sem):
    return pltpu.make_async_copy(src.at[pl.ds(src_row, 1), :], dst.at[pl.ds(dst_row, 1), :], sem)


def _moe_scatter_kernel(ge_ref, pd_ref, slot_hbm, h_ref, xs_hbm, slot_smem, zero_ref, sem_ids, sem_rows, sem_zero):
    i = pl.program_id(0)

    def clear_copies(first_row):
        base = pl.multiple_of(first_row, ZR)
        return [pltpu.make_async_copy(zero_ref, xs_hbm.at[pl.ds(base + k * ZR, ZR), :], sem_zero)
                for k in range(TM_FFN // ZR)]

    def for_unused_tiles(fn):
        def body(t, carry):
            for cp in clear_copies(t * TM_FFN):
                fn(cp)
            return carry
        lax.fori_loop(ge_ref[N_EXPERTS - 1] // TM_FFN, MOE_TILES, body, 0)

    @pl.when(i == 0)
    def _():
        zero_ref[...] = jnp.zeros_like(zero_ref)
        for e in range(N_EXPERTS):
            @pl.when(pd_ref[e] > 0)
            def _():
                for cp in clear_copies(ge_ref[e] - TM_FFN):
                    cp.start()
        for_unused_tiles(lambda cp: cp.start())
        for e in range(N_EXPERTS):
            @pl.when(pd_ref[e] > 0)
            def _():
                for cp in clear_copies(ge_ref[e] - TM_FFN):
                    cp.wait()
        for_unused_tiles(lambda cp: cp.wait())

    ids_copy = pltpu.make_async_copy(slot_hbm.at[i], slot_smem, sem_ids)
    ids_copy.start()
    ids_copy.wait()

    def start(r, carry):
        _row_copy(h_ref, r, xs_hbm, slot_smem[TOP_K * r], sem_rows).start()
        _row_copy(h_ref, r, xs_hbm, slot_smem[TOP_K * r + 1], sem_rows).start()
        return carry

    lax.fori_loop(0, TR, start, 0)

    def wait(r, carry):
        _row_copy(h_ref, r, xs_hbm, 0, sem_rows).wait()
        _row_copy(h_ref, r, xs_hbm, 0, sem_rows).wait()
        return carry

    lax.fori_loop(0, TR, wait, 0)


def _moe_scatter(h, slot, g_end, padded):
    grid_spec = pltpu.PrefetchScalarGridSpec(
        num_scalar_prefetch=2,
        grid=(T // TR,),
        in_specs=[pl.BlockSpec(memory_space=pl.ANY),
                  pl.BlockSpec((TR, D_MODEL), lambda i, ge, pd: (i, 0))],
        out_specs=pl.BlockSpec(memory_space=pl.ANY),
        scratch_shapes=[pltpu.SMEM((TOP_K * TR,), jnp.int32),
                        pltpu.VMEM((ZR, D_MODEL), F32),
                        pltpu.SemaphoreType.DMA,
                        pltpu.SemaphoreType.DMA,
                        pltpu.SemaphoreType.DMA],
    )
    return pl.pallas_call(
        _moe_scatter_kernel,
        grid_spec=grid_spec,
        out_shape=jax.ShapeDtypeStruct((MOE_TILES * TM_FFN, D_MODEL), F32),
        compiler_params=_cparams(("arbitrary",)),
        name="moe_scatter",
    )(g_end, padded, slot.reshape(T // TR, TOP_K * TR), h)


def _ffn_moe_kernel(te_ref, nu_ref, xs_hbm, wg_ref, wu_ref, wd_ref, o_ref, xg_ref, xb_ref, sem):
    i = pl.program_id(0)
    j = pl.program_id(1)
    n_used = nu_ref[0]
    used = i < n_used

    def tile_copy(t):
        return pltpu.make_async_copy(xs_hbm.at[pl.ds(pl.multiple_of(t * TM_FFN, TM_FFN), TM_FFN), :], xg_ref, sem)

    @pl.when(used & (j == 0))
    def _():
        @pl.when(i == 0)
        def _():
            tile_copy(0).start()

        tile_copy(i).wait()
        xb_ref[...] = xg_ref[...].astype(BF16)

        @pl.when(i + 1 < n_used)
        def _():
            tile_copy(i + 1).start()

    @pl.when(used)
    def _():
        _accumulate(o_ref, _swiglu_step(xb_ref[...], wg_ref, wu_ref, wd_ref), j)

    @pl.when(jnp.logical_not(used) & (j == 0))
    def _():
        o_ref[...] = jnp.zeros_like(o_ref)


def _ffn_moe(xs, tile_e, n_used, wg, wu, wd):
    def hid(i, j, te, nu):
        return jnp.where(i < nu[0], j, NF - 1)

    grid_spec = pltpu.PrefetchScalarGridSpec(
        num_scalar_prefetch=2,
        grid=(MOE_TILES, NF),
        in_specs=[pl.BlockSpec(memory_space=pl.ANY),
                  pl.BlockSpec((None, D_MODEL, TF), lambda i, j, te, nu: (te[i], 0, hid(i, j, te, nu))),
                  pl.BlockSpec((None, D_MODEL, TF), lambda i, j, te, nu: (te[i], 0, hid(i, j, te, nu))),
                  pl.BlockSpec((None, TF, D_MODEL), lambda i, j, te, nu: (te[i], hid(i, j, te, nu), 0))],
        out_specs=pl.BlockSpec((TM_FFN, D_MODEL), lambda i, j, te, nu: (i, 0)),
        scratch_shapes=[pltpu.VMEM((TM_FFN, D_MODEL), F32),
                        pltpu.VMEM((TM_FFN, D_MODEL), BF16),
                        pltpu.SemaphoreType.DMA],
    )
    return pl.pallas_call(
        _ffn_moe_kernel,
        grid_spec=grid_spec,
        out_shape=jax.ShapeDtypeStruct((MOE_TILES * TM_FFN, D_MODEL), F32),
        compiler_params=_cparams(("arbitrary", "arbitrary")),
        name="ffn_moe",
    )(tile_e, n_used, xs, wg, wu, wd)


def _combine_kernel(slot_hbm, ys_hbm, wt_ref, x_ref, gate_ref, g_ref, o_ref, slot_smem, y0_ref, y1_ref, sem_ids, sem_rows):
    i = pl.program_id(0)
    ids_copy = pltpu.make_async_copy(slot_hbm.at[i], slot_smem, sem_ids)
    ids_copy.start()
    ids_copy.wait()

    def start(r, carry):
        _row_copy(ys_hbm, slot_smem[TOP_K * r], y0_ref, r, sem_rows).start()
        _row_copy(ys_hbm, slot_smem[TOP_K * r + 1], y1_ref, r, sem_rows).start()
        return carry

    lax.fori_loop(0, TC, start, 0)

    def wait(r, carry):
        _row_copy(ys_hbm, 0, y0_ref, r, sem_rows).wait()
        _row_copy(ys_hbm, 0, y1_ref, r, sem_rows).wait()
        return carry

    lax.fori_loop(0, TC, wait, 0)
    wt = wt_ref[...]
    y = wt[:, 0:1] * y0_ref[...] + wt[:, 1:2] * y1_ref[...]
    o_ref[...] = _rms(x_ref[...] + gate_ref[...] * y, g_ref[...])


def _combine(slot, ys, wt, x, mods, k_gate, g):
    return pl.pallas_call(
        _combine_kernel,
        grid=(T // TC,),
        in_specs=[pl.BlockSpec(memory_space=pl.ANY),
                  pl.BlockSpec(memory_space=pl.ANY),
                  pl.BlockSpec((TC, LANES), lambda i: (i, 0)),
                  pl.BlockSpec((TC, D_MODEL), lambda i: (i, 0)),
                  pl.BlockSpec((None, 1, D_MODEL), lambda i: (_cond_row(i, TC), 0, k_gate)),
                  pl.BlockSpec((1, D_MODEL), lambda i: (0, 0))],
        out_specs=pl.BlockSpec((TC, D_MODEL), lambda i: (i, 0)),
        out_shape=jax.ShapeDtypeStruct((T, D_MODEL), F32),
        scratch_shapes=[pltpu.SMEM((TOP_K * TC,), jnp.int32),
                        pltpu.VMEM((TC, D_MODEL), F32),
                        pltpu.VMEM((TC, D_MODEL), F32),
                        pltpu.SemaphoreType.DMA,
                        pltpu.SemaphoreType.DMA],
        compiler_params=_cparams(("arbitrary",)),
        name="moe_combine",
    )(slot.reshape(T // TC, TOP_K * TC), ys, wt, x, mods, g.reshape(1, D_MODEL))


def kernel(x_prompt, x_sample, cache_attn_k, cache_attn_v, cache_mla_ckv, cache_mla_krope, c, c_ctx, ln1_g, ln2_g, w_ada, b_ada, w_in, attn_sink, mla_q_norm_g, w_uq, mla_kv_norm_g, w_ukv, pool_w, pool_scale, w_branch_a, w_branch_b, w_branch_c, w_out, ffn_w_gate, ffn_w_up, ffn_w_down, router_w, moe_w_gate, moe_w_up, moe_w_down, final_g):
    x = jnp.concatenate([x_prompt.reshape(TP, D_MODEL), x_sample.reshape(TS, D_MODEL)], axis=0)
    cond = jnp.concatenate([c_ctx[None, :], c, jnp.zeros((N_COND - 1 - DEC_BATCH, D_MODEL), F32)], axis=0)
    rows = DEC_SEQ // GRID_W
    cos_a, sin_a = _rope_tables(rows, HD_A)
    cos_b, sin_b = _rope_tables(rows, QK_ROPE)
    st_k, st_v, st_ckv, st_kr = [], [], [], []
    assert DEPTH == 2, "layer 0 is the dense-FFN layer, layer 1 the expert layer that ends the trunk"
    all_mods = [_adaln(cond, w_ada[l], b_ada[l]).reshape(N_COND, 1, 6 * D_MODEL) for l in range(DEPTH)]
    h1 = _norm_mod(x, ln1_g[0], all_mods[0], 0, 1, BF16, "norm_mod1")
    for l in range(DEPTH):
        mods = all_mods[l]
        wl = w_in[l]
        qa = _matmul(h1, wl[:, :OFF_K].astype(BF16), F32, TM, A_W, "proj_q")
        kv = _matmul(h1, wl[:, OFF_K:OFF_CQ].astype(BF16), F32, TM, 2 * KV_W, "proj_kv")
        zb = _matmul(h1, wl[:, OFF_CQ:OFF_POOL].astype(BF16), F32, TM, OFF_POOL - OFF_CQ, "proj_mla")
        u = _matmul(h1, wl[:, OFF_POOL:OFF_GATE].astype(BF16), F32, TM, C_W, "proj_pool")
        gates = _matmul(h1, wl[:, OFF_GATE:].astype(BF16), BF16, TM, 1024, "proj_gates")
        sink = attn_sink[l]
        ck = cache_attn_k[:, l].reshape(DEC_BATCH, PAST_LEN, KV_W)
        cv = cache_attn_v[:, l].reshape(DEC_BATCH, PAST_LEN, KV_W)
        oa = jnp.concatenate([_attn_a_prompt(qa, kv, sink),
                              _attn_a_sample(qa, kv, ck, cv, sink, cos_a, sin_a)], axis=0)
        wq = w_uq[l].reshape(Q_LORA, N_HEADS_B, QK_NOPE + QK_ROPE)
        wq_n = wq[:, :, :QK_NOPE].reshape(Q_LORA, N_HEADS_B * QK_NOPE).astype(BF16)
        wq_r = wq[:, :, QK_NOPE:].reshape(Q_LORA, N_HEADS_B * QK_ROPE).astype(BF16)
        wkv = w_ukv[l].reshape(KV_LORA, N_HEADS_B, QK_NOPE + V_HD)
        wkv = jnp.concatenate([wkv[:, :, :QK_NOPE].reshape(KV_LORA, B_W),
                               wkv[:, :, QK_NOPE:].reshape(KV_LORA, B_W)], axis=1).astype(BF16)
        qn, qr, ckv, kr = _mla_prep(zb, mla_q_norm_g[l], mla_kv_norm_g[l], wq_n, wq_r)
        kv_p = _matmul(ckv[:TP], wkv, BF16, TM, 2 * B_W, "mla_kv_prompt")
        ckv_all = jnp.concatenate([ckv[TP:].reshape(DEC_BATCH, DEC_SEQ, KV_LORA), cache_mla_ckv[:, l]], axis=1)
        kv_s = _matmul(ckv_all.reshape(DEC_BATCH * MLA_KEYS, KV_LORA), wkv, BF16, TM, 2 * B_W, "mla_kv_sample")
        kr_all = jnp.concatenate([_rope_k(kr, cos_b, sin_b).reshape(DEC_BATCH, DEC_SEQ, QK_ROPE),
                                  cache_mla_krope[:, l].astype(BF16)], axis=1)
        ob = jnp.concatenate([_mla_prompt(qn, qr, kv_p, kr),
                              _mla_sample(qn, qr, kv_s, kr_all, cos_b, sin_b)], axis=0)
        pw = pool_w[l].astype(BF16)
        oc = jnp.concatenate([_pool(u, pw, pool_scale[l], SEQ, BATCH, 0, "pool_prompt"),
                              _pool(u, pw, pool_scale[l], DEC_SEQ, DEC_BATCH, TP // DEC_SEQ, "pool_sample")], axis=0)
        y = _merge(oa, ob, oc, gates, w_branch_a[l].astype(BF16), w_branch_b[l].astype(BF16),
                   w_branch_c[l].astype(BF16))
        st_k.append(kv[:TP, :KV_W].reshape(BATCH, SEQ, N_KV_A, HD_A))
        st_v.append(kv[:TP, KV_W:].reshape(BATCH, SEQ, N_KV_A, HD_A))
        st_ckv.append(ckv[:TP].reshape(BATCH, SEQ, KV_LORA))
        st_kr.append(kr[:TP].reshape(BATCH, SEQ, QK_ROPE))
        if l == 0:
            x, h2 = _out_proj(y, w_out[l].astype(BF16), x, mods, ln2_g[l], True)
            f = _ffn_dense(h2, ffn_w_gate[0].astype(BF16), ffn_w_up[0].astype(BF16), ffn_w_down[0].astype(BF16))
            x, h1 = _resid_norm(x, f, mods, 5, ln1_g[1], all_mods[1], 0, 1)
        else:
            (x,) = _out_proj(y, w_out[l].astype(BF16), x, mods, ln2_g[l], False)
            rw = jnp.pad(router_w[0], ((0, 0), (0, LANES - N_EXPERTS)))
            rw_hi = rw.astype(BF16)
            rw_lo = (rw - rw_hi.astype(F32)).astype(BF16)
            h2, meta, wt, counts = _norm_router(x, ln2_g[l], mods, rw_hi, rw_lo)
            slot, g_end, padded, tile_e, n_used = _dispatch(meta, counts)
            xs = _moe_scatter(h2, slot, g_end, padded)
            ys = _ffn_moe(xs, tile_e, n_used, moe_w_gate[0].astype(BF16), moe_w_up[0].astype(BF16),
                          moe_w_down[0].astype(BF16))
            y = _combine(slot, ys, wt, x, mods, 5, final_g)
    return (y[:TP].reshape(BATCH, SEQ, D_MODEL), y[TP:].reshape(DEC_BATCH, DEC_SEQ, D_MODEL),
            jnp.stack(st_k, axis=1), jnp.stack(st_v, axis=1), jnp.stack(st_ckv, axis=1), jnp.stack(st_kr, axis=1))
```

```python
import functools
import math

import jax
import jax.numpy as jnp
from jax import lax
from jax.experimental import pallas as pl
from jax.experimental.pallas import tpu as pltpu

BF16 = jnp.bfloat16
F32 = jnp.float32

D_MODEL = 2048
BATCH = 16
SEQ = 256
DEPTH = 2
DEC_BATCH = 8
DEC_SEQ = 2048
PAST_LEN = 512
GRID_W = 64
BLK = 128
EPS = 1e-6
ROPE_BASE = 10000.0
NEG_INF = -1e30
N_HEADS_A = 16
N_KV_A = 2
HD_A = 64
WINDOW = 128
A_W = N_HEADS_A * HD_A
KV_W = N_KV_A * HD_A
N_HEADS_B = 8
Q_LORA = 512
KV_LORA = 256
QK_NOPE = 128
QK_ROPE = 64
V_HD = 128
B_W = N_HEADS_B * V_HD
POOL_WINDOWS = (2, 4, 8, 16)
POOL_GROUPS = 4
POOL_GW = 256
C_W = POOL_GROUPS * POOL_GW
OFF_K = A_W
OFF_V = OFF_K + KV_W
OFF_CQ = OFF_V + KV_W
OFF_CKV = OFF_CQ + Q_LORA
OFF_KR = OFF_CKV + KV_LORA
OFF_POOL = OFF_KR + QK_ROPE
OFF_GATE = OFF_POOL + C_W
IN_COLS = OFF_GATE + 3 * D_MODEL
D_FF = 5632
N_EXPERTS = 8
TOP_K = 2

TP = BATCH * SEQ
TS = DEC_BATCH * DEC_SEQ
T = TP + TS
N_COND = 16
MLA_KEYS = DEC_SEQ + PAST_LEN
LANES = 128
POOL_HALO = 8
LOG2E = math.log2(math.e)
MLA_HEAD_K = 256
MLA_AUG = KV_LORA + QK_ROPE

VMEM_LIMIT = 56 * 1024 * 1024

TM = 1024
TM_FFN = 1024
TF = 512
NF = D_FF // TF
MOE_TILES = (TOP_K * T) // TM_FFN + N_EXPERTS
TR = 512
ZR = 256
TC = 256
TQ_MLA = 512


def _cparams(sem):
    return pltpu.CompilerParams(dimension_semantics=sem, vmem_limit_bytes=VMEM_LIMIT)


def _cond_row(i, tm):
    n_p = TP // tm
    per_b = DEC_SEQ // tm
    return jnp.where(i < n_p, 0, (i - n_p) // per_b + 1)


def _dot(a, b):
    return jnp.dot(a, b, preferred_element_type=F32)


def _dot_nt(a, b):
    return lax.dot_general(a, b, (((1,), (1,)), ((), ())), preferred_element_type=F32)


def _mm_kernel(x_ref, w_ref, o_ref):
    o_ref[...] = _dot(x_ref[...].astype(BF16), w_ref[...].astype(BF16)).astype(o_ref.dtype)


def _matmul(x, w, out_dtype, tm, tn, name):
    m, k = x.shape
    n = w.shape[1]
    return pl.pallas_call(
        _mm_kernel,
        grid=(m // tm, n // tn),
        in_specs=[pl.BlockSpec((tm, k), lambda i, j: (i, 0)),
                  pl.BlockSpec((k, tn), lambda i, j: (0, j))],
        out_specs=pl.BlockSpec((tm, tn), lambda i, j: (i, j)),
        out_shape=jax.ShapeDtypeStruct((m, n), out_dtype),
        compiler_params=_cparams(("parallel", "arbitrary")),
        name=name,
    )(x, w)


def _mm_wcast_kernel(x_ref, w_ref, o_ref, wb_ref):
    @pl.when(pl.program_id(1) == 0)
    def _():
        wb_ref[...] = w_ref[...].astype(BF16)

    o_ref[...] = _dot(x_ref[...], wb_ref[...]).astype(o_ref.dtype)


def _matmul_wcast(x, w, out_dtype, tn, name, layer=None, col0=0, n=None):
    m, k = x.shape
    n = w.shape[-1] if n is None else n
    tm = TM
    assert col0 % tn == 0 and n % tn == 0
    c0 = col0 // tn
    if layer is None:
        w_spec = pl.BlockSpec((k, tn), lambda j, i: (0, c0 + j))
    else:
        w_spec = pl.BlockSpec((None, k, tn), lambda j, i: (layer, 0, c0 + j))
    return pl.pallas_call(
        _mm_wcast_kernel,
        grid=(n // tn, m // tm),
        in_specs=[pl.BlockSpec((tm, k), lambda j, i: (i, 0)), w_spec],
        out_specs=pl.BlockSpec((tm, tn), lambda j, i: (i, j)),
        out_shape=jax.ShapeDtypeStruct((m, n), out_dtype),
        scratch_shapes=[pltpu.VMEM((k, tn), BF16)],
        compiler_params=_cparams(("parallel", "arbitrary")),
        name=name,
    )(x, w)


def _ada_kernel(c_ref, w_ref, b_ref, o_ref):
    c = c_ref[...]
    a = (c * jax.nn.sigmoid(c)).astype(BF16)
    o_ref[...] = _dot(a, w_ref[...].astype(BF16)) + b_ref[...]


def _adaln(cond, w, b, layer):
    n = w.shape[2]
    tn = 1024
    return pl.pallas_call(
        _ada_kernel,
        grid=(n // tn,),
        in_specs=[pl.BlockSpec((N_COND, D_MODEL), lambda j: (0, 0)),
                  pl.BlockSpec((None, D_MODEL, tn), lambda j: (layer, 0, j)),
                  pl.BlockSpec((None, 1, tn), lambda j: (layer, 0, j))],
        out_specs=pl.BlockSpec((N_COND, tn), lambda j: (0, j)),
        out_shape=jax.ShapeDtypeStruct((N_COND, n), F32),
        compiler_params=_cparams(("arbitrary",)),
        name="adaln",
    )(cond, w, b.reshape(DEPTH, 1, n))


def _norm_mod_kernel(x_ref, g_ref, sh_ref, sc_ref, o_ref):
    x = x_ref[...]
    y = x * lax.rsqrt(jnp.mean(x * x, axis=-1, keepdims=True) + EPS) * g_ref[...]
    o_ref[...] = (y * (1 + sc_ref[...]) + sh_ref[...]).astype(o_ref.dtype)


def _norm_mod(x, g, mods, k_shift, k_scale, out_dtype, name):
    tm = 512
    return pl.pallas_call(
        _norm_mod_kernel,
        grid=(T // tm,),
        in_specs=[pl.BlockSpec((tm, D_MODEL), lambda i: (i, 0)),
                  pl.BlockSpec((1, D_MODEL), lambda i: (0, 0)),
                  pl.BlockSpec((None, 1, D_MODEL), lambda i: (_cond_row(i, tm), 0, k_shift)),
                  pl.BlockSpec((None, 1, D_MODEL), lambda i: (_cond_row(i, tm), 0, k_scale))],
        out_specs=pl.BlockSpec((tm, D_MODEL), lambda i: (i, 0)),
        out_shape=jax.ShapeDtypeStruct((T, D_MODEL), out_dtype),
        compiler_params=_cparams(("parallel",)),
        name=name,
    )(x, g.reshape(1, D_MODEL), mods, mods)


def _rms(x, g):
    return x * lax.rsqrt(jnp.mean(x * x, axis=-1, keepdims=True) + EPS) * g


def _resid_norm_kernel(x_ref, f_ref, gate_ref, g_ref, sh_ref, sc_ref, x2_ref, h_ref):
    x2 = x_ref[...] + gate_ref[...] * f_ref[...]
    x2_ref[...] = x2
    h_ref[...] = (_rms(x2, g_ref[...]) * (1 + sc_ref[...]) + sh_ref[...]).astype(h_ref.dtype)


def _resid_norm(x, f, mods, k_gate, g, mods_next, k_shift, k_scale):
    tm = 512
    rows = pl.BlockSpec((tm, D_MODEL), lambda i: (i, 0))
    mod = lambda k: pl.BlockSpec((None, 1, D_MODEL), lambda i: (_cond_row(i, tm), 0, k))
    return pl.pallas_call(
        _resid_norm_kernel,
        grid=(T // tm,),
        in_specs=[rows, rows, mod(k_gate), pl.BlockSpec((1, D_MODEL), lambda i: (0, 0)), mod(k_shift), mod(k_scale)],
        out_specs=[rows, rows],
        out_shape=[jax.ShapeDtypeStruct((T, D_MODEL), F32), jax.ShapeDtypeStruct((T, D_MODEL), BF16)],
        compiler_params=_cparams(("parallel",)),
        name="resid_norm",
    )(x, f, mods, g.reshape(1, D_MODEL), mods_next, mods_next)


def _rope_tables(rows, dim):
    quarter = dim // 4
    inv = ROPE_BASE ** (-jnp.arange(quarter, dtype=F32) / quarter)
    r = jnp.repeat(jnp.arange(rows, dtype=F32), GRID_W)
    col = jnp.tile(jnp.arange(GRID_W, dtype=F32), rows)
    ar, ac = r[:, None] * inv, col[:, None] * inv
    cos = jnp.concatenate([jnp.cos(ar), jnp.cos(ar), jnp.cos(ac), jnp.cos(ac)], axis=-1)
    sin = jnp.concatenate([-jnp.sin(ar), jnp.sin(ar), -jnp.sin(ac), jnp.sin(ac)], axis=-1)
    reps = LANES // dim
    return jnp.tile(cos, (1, reps)), jnp.tile(sin, (1, reps))


def _rope_lanes(x, cos, sin, quarter):
    lane = lax.broadcasted_iota(jnp.int32, (x.shape[0], LANES), 1)
    first = (lane % (2 * quarter)) < quarter
    outs = []
    for c in range(x.shape[1] // LANES):
        xc = x[:, c * LANES:(c + 1) * LANES]
        partner = jnp.where(first, pltpu.roll(xc, LANES - quarter, 1), pltpu.roll(xc, quarter, 1))
        outs.append(xc * cos + partner * sin)
    return outs[0] if len(outs) == 1 else jnp.concatenate(outs, axis=1)


def _attn_a_kernel(*refs, local, tq, n_blocks):
    assert N_KV_A * HD_A == LANES
    if local:
        (sink_ref, q_ref, kc_ref, vc_ref, kvm_ref, kv0_ref, kvp_ref,
         cq_ref, sq_ref, ckm_ref, skm_ref, ck0_ref, sk0_ref, ckp_ref, skp_ref, o_ref) = refs
    else:
        sink_ref, q_ref, kc_ref, vc_ref, o_ref = refs
    quarter = HD_A // 4
    q = q_ref[...]
    k_all = kc_ref[...]
    v_all = vc_ref[...]
    n_loc = 0
    if local:
        j = pl.program_id(1)
        n_loc = 3 * BLK
        q = _rope_lanes(q, cq_ref[...], sq_ref[...], quarter)
        k_all = jnp.concatenate([
            _rope_lanes(kvm_ref[:, :KV_W], ckm_ref[...], skm_ref[...], quarter),
            _rope_lanes(kv0_ref[:, :KV_W], ck0_ref[...], sk0_ref[...], quarter),
            _rope_lanes(kvp_ref[:, :KV_W], ckp_ref[...], skp_ref[...], quarter), k_all], axis=0)
        v_all = jnp.concatenate([kvm_ref[:, KV_W:], kv0_ref[:, KV_W:], kvp_ref[:, KV_W:], v_all], axis=0)
        qi = lax.broadcasted_iota(jnp.int32, (tq, n_loc), 0)
        ki = lax.broadcasted_iota(jnp.int32, (tq, n_loc), 1)
        k_lo = jnp.where(j == 0, BLK, 0)
        k_hi = jnp.where(j == n_blocks - 1, 2 * BLK, 3 * BLK)
        valid = (ki >= qi) & (ki <= qi + 2 * WINDOW) & (ki >= k_lo) & (ki < k_hi)
        bias = jnp.where(valid, 0.0, NEG_INF)
    q = (q * (HD_A ** -0.5 * LOG2E)).astype(BF16)
    nk = k_all.shape[0]
    low_k = lax.broadcasted_iota(jnp.int32, (nk, LANES), 1) < HD_A
    low_q = lax.broadcasted_iota(jnp.int32, (tq, LANES), 1) < HD_A
    rep = N_HEADS_A // N_KV_A

    def blockdiag(x, g):
        swapped = pltpu.roll(x, HD_A, 1)
        lo_src, hi_src = (x, swapped) if g == 0 else (swapped, x)
        return jnp.concatenate([jnp.where(low_k, lo_src, 0.0), jnp.where(low_k, 0.0, hi_src)], axis=0).astype(BF16)

    for g in range(N_KV_A):
        k2 = blockdiag(k_all, g)
        v2 = blockdiag(v_all, g)
        for p in range(rep // 2):
            h0 = g * rep + 2 * p
            s = _dot_nt(q[:, h0 * HD_A:h0 * HD_A + LANES], k2)
            parts, inv = [], []
            for t in range(2):
                sh = s[:, t * nk:(t + 1) * nk]
                sk = sink_ref[h0 + t] * LOG2E
                if local:
                    s_loc = sh[:, :n_loc] + bias
                    s_ctx = sh[:, n_loc:]
                    m = jnp.maximum(jnp.maximum(jnp.max(s_loc, axis=-1, keepdims=True),
                                                jnp.max(s_ctx, axis=-1, keepdims=True)), sk)
                    p_loc = jnp.exp2(s_loc - m)
                    p_ctx = jnp.exp2(s_ctx - m)
                    denom = (jnp.sum(p_loc, axis=-1, keepdims=True) + jnp.sum(p_ctx, axis=-1, keepdims=True)
                             + jnp.exp2(sk - m))
                    parts += [p_loc, p_ctx]
                else:
                    m = jnp.maximum(jnp.max(sh, axis=-1, keepdims=True), sk)
                    p_all = jnp.exp2(sh - m)
                    denom = jnp.sum(p_all, axis=-1, keepdims=True) + jnp.exp2(sk - m)
                    parts.append(p_all)
                inv.append(1.0 / denom)
            o2 = _dot(jnp.concatenate(parts, axis=1).astype(BF16), v2) * jnp.where(low_q, inv[0], inv[1])
            o_ref[:, h0 * HD_A:h0 * HD_A + LANES] = o2.astype(o_ref.dtype)


def _attn_a_prompt(qa, kv, sink):
    kern = functools.partial(_attn_a_kernel, local=False, tq=SEQ, n_blocks=1)
    return pl.pallas_call(
        kern,
        grid=(BATCH,),
        in_specs=[pl.BlockSpec(memory_space=pltpu.SMEM),
                  pl.BlockSpec((SEQ, A_W), lambda b: (b, 0)),
                  pl.BlockSpec((SEQ, KV_W), lambda b: (b, 0)),
                  pl.BlockSpec((SEQ, KV_W), lambda b: (b, 1))],
        out_specs=pl.BlockSpec((SEQ, A_W), lambda b: (b, 0)),
        out_shape=jax.ShapeDtypeStruct((TP, A_W), BF16),
        compiler_params=_cparams(("parallel",)),
        name="attn_a_prompt",
    )(sink, qa, kv, kv)


def _attn_a_sample(qa, kv, ck, cv, sink, cos, sin):
    nb = DEC_SEQ // BLK
    base = TP // BLK

    def row(b, j):
        return base + b * nb + j

    def tab_prev(b, j):
        return (jnp.maximum(j - 1, 0), 0)

    def tab_next(b, j):
        return (jnp.minimum(j + 1, nb - 1), 0)

    kv_spec = lambda f: pl.BlockSpec((BLK, 2 * KV_W), f)
    tab_spec = lambda f: pl.BlockSpec((BLK, LANES), f)
    kern = functools.partial(_attn_a_kernel, local=True, tq=BLK, n_blocks=nb)
    return pl.pallas_call(
        kern,
        grid=(DEC_BATCH, nb),
        in_specs=[pl.BlockSpec(memory_space=pltpu.SMEM),
                  pl.BlockSpec((BLK, A_W), lambda b, j: (row(b, j), 0)),
                  pl.BlockSpec((None, PAST_LEN, KV_W), lambda b, j: (b, 0, 0)),
                  pl.BlockSpec((None, PAST_LEN, KV_W), lambda b, j: (b, 0, 0)),
                  kv_spec(lambda b, j: (row(b, jnp.maximum(j - 1, 0)), 0)),
                  kv_spec(lambda b, j: (row(b, j), 0)),
                  kv_spec(lambda b, j: (row(b, jnp.minimum(j + 1, nb - 1)), 0)),
                  tab_spec(lambda b, j: (j, 0)), tab_spec(lambda b, j: (j, 0)),
                  tab_spec(tab_prev), tab_spec(tab_prev),
                  tab_spec(lambda b, j: (j, 0)), tab_spec(lambda b, j: (j, 0)),
                  tab_spec(tab_next), tab_spec(tab_next)],
        out_specs=pl.BlockSpec((BLK, A_W), lambda b, j: (b * nb + j, 0)),
        out_shape=jax.ShapeDtypeStruct((TS, A_W), BF16),
        compiler_params=_cparams(("parallel", "arbitrary")),
        name="attn_a_sample",
    )(sink, qa, ck, cv, kv, kv, kv, cos, sin, cos, sin, cos, sin, cos, sin)


def _mla_prep_kernel(z_ref, gq_ref, gkv_ref, wq_ref, cos_ref, sin_ref, q_ref, ckv_ref, kr_ref, xa_ref):
    quarter = QK_ROPE // 4
    c = (QK_NOPE + QK_ROPE) ** -0.5 * LOG2E
    cos = cos_ref[...]
    sin = sin_ref[...]
    cq = z_ref[:, :Q_LORA]
    q = _dot(_rms(cq, gq_ref[...]).astype(BF16), wq_ref[...])
    for h in range(N_HEADS_B):
        lo = h * MLA_HEAD_K
        q_ref[:, lo:lo + QK_NOPE] = (q[:, lo:lo + QK_NOPE] * c).astype(q_ref.dtype)
        rot = _rope_lanes(q[:, lo + QK_NOPE:lo + MLA_HEAD_K], cos, sin, quarter)
        q_ref[:, lo + QK_NOPE:lo + MLA_HEAD_K] = (rot * c).astype(q_ref.dtype)
    ckv = _rms(z_ref[:, Q_LORA:Q_LORA + KV_LORA], gkv_ref[...])
    ckv_ref[...] = ckv
    kr = z_ref[:, Q_LORA + KV_LORA:]
    kr_ref[...] = kr
    xa_ref[:, :KV_LORA] = ckv.astype(xa_ref.dtype)
    kr_rot = _rope_lanes(jnp.concatenate([kr, kr], axis=1), cos, sin, quarter)
    xa_ref[:, KV_LORA:] = kr_rot[:, :QK_ROPE].astype(xa_ref.dtype)


def _mla_prep(z, gq, gkv, wq_cat, cos, sin):
    tm = TM
    zc = Q_LORA + KV_LORA + QK_ROPE
    n_p = TP // tm
    per_b = DEC_SEQ // tm
    full = lambda shape: pl.BlockSpec(shape, lambda i: (0, 0))
    rows = lambda c: pl.BlockSpec((tm, c), lambda i: (i, 0))
    tab = pl.BlockSpec((tm, LANES), lambda i: (jnp.where(i < n_p, 0, 1 + (i - n_p) % per_b), 0))
    return pl.pallas_call(
        _mla_prep_kernel,
        grid=(T // tm,),
        in_specs=[rows(zc), full((1, Q_LORA)), full((1, KV_LORA)), full((Q_LORA, N_HEADS_B * MLA_HEAD_K)), tab, tab],
        out_specs=[rows(N_HEADS_B * MLA_HEAD_K), rows(KV_LORA), rows(QK_ROPE), rows(MLA_AUG)],
        out_shape=[jax.ShapeDtypeStruct((T, N_HEADS_B * MLA_HEAD_K), BF16),
                   jax.ShapeDtypeStruct((T, KV_LORA), F32),
                   jax.ShapeDtypeStruct((T, QK_ROPE), F32),
                   jax.ShapeDtypeStruct((T, MLA_AUG), BF16)],
        compiler_params=_cparams(("parallel",)),
        name="mla_prep",
    )(z, gq.reshape(1, Q_LORA), gkv.reshape(1, KV_LORA), wq_cat, cos, sin)


def _mla_weights(w_uq, w_ukv):
    wq = w_uq.reshape(Q_LORA, N_HEADS_B, QK_NOPE + QK_ROPE)
    wq = jnp.pad(wq, ((0, 0), (0, 0), (0, MLA_HEAD_K - QK_NOPE - QK_ROPE))).reshape(Q_LORA, N_HEADS_B * MLA_HEAD_K)
    wkv = w_ukv.reshape(KV_LORA, N_HEADS_B, QK_NOPE + V_HD)
    wk = jnp.pad(wkv[:, :, :QK_NOPE], ((0, QK_ROPE), (0, 0), (0, MLA_HEAD_K - QK_NOPE)))
    eye = jnp.pad(jnp.eye(QK_ROPE, dtype=F32), ((KV_LORA, 0), (QK_NOPE, MLA_HEAD_K - QK_NOPE - QK_ROPE)))
    wk = (wk + eye[:, None, :]).reshape(MLA_AUG, N_HEADS_B * MLA_HEAD_K)
    wv = jnp.pad(wkv[:, :, QK_NOPE:], ((0, QK_ROPE), (0, 0), (0, 0))).reshape(MLA_AUG, B_W)
    return wq.astype(BF16), jnp.concatenate([wk, wv], axis=1).astype(BF16)


def _mla_kernel(q_ref, k_ref, v_ref, o_ref):
    for h in range(N_HEADS_B):
        ks = slice(h * MLA_HEAD_K, (h + 1) * MLA_HEAD_K)
        vs = slice(h * V_HD, (h + 1) * V_HD)
        s = _dot_nt(q_ref[:, ks], k_ref[:, ks])
        p = jnp.exp2(s - jnp.max(s, axis=-1, keepdims=True))
        denom = jnp.sum(p, axis=-1, keepdims=True)
        o_ref[:, vs] = (_dot(p.astype(BF16), v_ref[:, vs]) / denom).astype(o_ref.dtype)


def _mla_attend(q, kv, tq, q_base, n_seq, nq, nk, name):
    kw = N_HEADS_B * MLA_HEAD_K
    return pl.pallas_call(
        _mla_kernel,
        grid=(n_seq, nq),
        in_specs=[pl.BlockSpec((tq, kw), lambda b, i: (q_base + b * nq + i, 0)),
                  pl.BlockSpec((nk, kw), lambda b, i: (b, 0), pipeline_mode=pl.Buffered(1)),
                  pl.BlockSpec((nk, B_W), lambda b, i: (b, kw // B_W), pipeline_mode=pl.Buffered(1))],
        out_specs=pl.BlockSpec((tq, B_W), lambda b, i: (b * nq + i, 0)),
        out_shape=jax.ShapeDtypeStruct((n_seq * nq * tq, B_W), BF16),
        compiler_params=_cparams(("parallel", "arbitrary")),
        name=name,
    )(q, kv, kv)


def _pool_kernel(u_ref, w_ref, s_ref, o_ref, pad_ref, *, n):
    chunk = min(n, 256)
    zeros = jnp.zeros((POOL_HALO, C_W), F32)
    pad_ref[0:POOL_HALO, :] = zeros
    pad_ref[POOL_HALO + n:POOL_HALO + n + POOL_HALO, :] = zeros
    pad_ref[POOL_HALO:POOL_HALO + n, :] = u_ref[...]
    for c in range(n // chunk):
        r0 = c * chunk
        t = lax.broadcasted_iota(jnp.int32, (chunk, 1), 0) + r0
        for g, win in enumerate(POOL_WINDOWS):
            left = win // 2
            right = win - left - 1
            cols = slice(g * POOL_GW, (g + 1) * POOL_GW)
            acc = pad_ref[POOL_HALO + r0 - left:POOL_HALO + r0 - left + chunk, cols]
            for k in range(-left + 1, right + 1):
                acc = acc + pad_ref[POOL_HALO + r0 + k:POOL_HALO + r0 + k + chunk, cols]
            cnt = (jnp.minimum(t + right, n - 1) + 1 - jnp.maximum(t - left, 0)).astype(F32)
            d = acc / cnt - pad_ref[POOL_HALO + r0:POOL_HALO + r0 + chunk, cols]
            y = _dot(d.astype(BF16), w_ref[g]) * s_ref[:, cols]
            o_ref[r0:r0 + chunk, cols] = y.astype(o_ref.dtype)


def _pool(u, w, scale, n, n_seq, row_base, name):
    kern = functools.partial(_pool_kernel, n=n)
    return pl.pallas_call(
        kern,
        grid=(n_seq,),
        in_specs=[pl.BlockSpec((n, C_W), lambda b: (row_base + b, 0)),
                  pl.BlockSpec((POOL_GROUPS, POOL_GW, POOL_GW), lambda b: (0, 0, 0)),
                  pl.BlockSpec((1, C_W), lambda b: (0, 0))],
        out_specs=pl.BlockSpec((n, C_W), lambda b: (b, 0)),
        out_shape=jax.ShapeDtypeStruct((n_seq * n, C_W), BF16),
        scratch_shapes=[pltpu.VMEM((n + 2 * POOL_HALO, C_W), F32)],
        compiler_params=_cparams(("parallel",)),
        name=name,
    )(u, w, scale.reshape(1, C_W))


def _merge_kernel(oap_ref, obp_ref, ocp_ref, oas_ref, obs_ref, ocs_ref, ga_ref, gb_ref, gc_ref,
                  wa_ref, wb_ref, wc_ref, o_ref, *, n_prompt_tiles):
    is_p = pl.program_id(0) < n_prompt_tiles
    pick = lambda p_ref, s_ref: jnp.where(is_p, p_ref[...], s_ref[...])
    y = jax.nn.sigmoid(ga_ref[...].astype(F32)) * _dot(pick(oap_ref, oas_ref), wa_ref[...])
    y = y + jax.nn.sigmoid(gb_ref[...].astype(F32)) * _dot(pick(obp_ref, obs_ref), wb_ref[...])
    y = y + jax.nn.sigmoid(gc_ref[...].astype(F32)) * _dot(pick(ocp_ref, ocs_ref), wc_ref[...])
    o_ref[...] = y.astype(o_ref.dtype)


def _merge(branches_p, branches_s, gates, wa, wb, wc):
    tm, tn = 512, 1024
    nn = D_MODEL // tn
    n_p = TP // tm
    rows_p = pl.BlockSpec((tm, A_W), lambda i, j: (jnp.minimum(i, n_p - 1), 0))
    rows_s = pl.BlockSpec((tm, A_W), lambda i, j: (jnp.maximum(i - n_p, 0), 0))
    gate = lambda k: pl.BlockSpec((tm, tn), lambda i, j: (i, k * nn + j))
    wspec = pl.BlockSpec((A_W, tn), lambda i, j: (0, j))
    return pl.pallas_call(
        functools.partial(_merge_kernel, n_prompt_tiles=n_p),
        grid=(T // tm, nn),
        in_specs=[rows_p] * 3 + [rows_s] * 3 + [gate(0), gate(1), gate(2), wspec, wspec, wspec],
        out_specs=pl.BlockSpec((tm, tn), lambda i, j: (i, j)),
        out_shape=jax.ShapeDtypeStruct((T, D_MODEL), BF16),
        compiler_params=_cparams(("parallel", "arbitrary")),
        name="merge",
    )(*branches_p, *branches_s, gates, gates, gates, wa, wb, wc)


def _out_proj_kernel(y_ref, w_ref, x_ref, gate_ref, g_ref, sh_ref, sc_ref, x1_ref, *h_ref):
    x1 = x_ref[...] + gate_ref[...] * _dot(y_ref[...], w_ref[...])
    x1_ref[...] = x1
    if h_ref:
        h_ref[0][...] = (_rms(x1, g_ref[...]) * (1 + sc_ref[...]) + sh_ref[...]).astype(h_ref[0].dtype)


def _out_proj(y, w, x, mods, g, emit_h):
    tm = 512
    rows = pl.BlockSpec((tm, D_MODEL), lambda i: (i, 0))
    mod = lambda k: pl.BlockSpec((None, 1, D_MODEL), lambda i: (_cond_row(i, tm), 0, k))
    n_out = 2 if emit_h else 1
    return pl.pallas_call(
        _out_proj_kernel,
        grid=(T // tm,),
        in_specs=[rows, pl.BlockSpec((D_MODEL, D_MODEL), lambda i: (0, 0)), rows, mod(2),
                  pl.BlockSpec((1, D_MODEL), lambda i: (0, 0)), mod(3), mod(4)],
        out_specs=[rows, rows][:n_out],
        out_shape=[jax.ShapeDtypeStruct((T, D_MODEL), F32), jax.ShapeDtypeStruct((T, D_MODEL), BF16)][:n_out],
        compiler_params=_cparams(("parallel",)),
        name="out_proj",
    )(y, w, x, mods, g.reshape(1, D_MODEL), mods, mods)


def _swiglu_step(x, wg_ref, wu_ref, wd_ref):
    g = _dot(x, wg_ref[...])
    u = _dot(x, wu_ref[...])
    h = (g * jax.nn.sigmoid(g) * u).astype(BF16)
    return _dot(h, wd_ref[...])


def _accumulate(o_ref, c, j):
    @pl.when(j == 0)
    def _():
        o_ref[...] = c

    @pl.when(j > 0)
    def _():
        o_ref[...] += c


def _ffn_dense_kernel(x_ref, wg_ref, wu_ref, wd_ref, o_ref):
    _accumulate(o_ref, _swiglu_step(x_ref[...], wg_ref, wu_ref, wd_ref), pl.program_id(1))


def _ffn_dense(h, wg, wu, wd):
    tm = TM_FFN
    return pl.pallas_call(
        _ffn_dense_kernel,
        grid=(T // tm, NF),
        in_specs=[pl.BlockSpec((tm, D_MODEL), lambda i, j: (i, 0)),
                  pl.BlockSpec((D_MODEL, TF), lambda i, j: (0, j)),
                  pl.BlockSpec((D_MODEL, TF), lambda i, j: (0, j)),
                  pl.BlockSpec((TF, D_MODEL), lambda i, j: (j, 0))],
        out_specs=pl.BlockSpec((tm, D_MODEL), lambda i, j: (i, 0)),
        out_shape=jax.ShapeDtypeStruct((T, D_MODEL), F32),
        compiler_params=_cparams(("parallel", "arbitrary")),
        name="ffn_dense",
    )(h, wg, wu, wd)


def _norm_router_kernel(x_ref, g_ref, sh_ref, sc_ref, whi_ref, wlo_ref, h_ref, meta_ref, wt_ref, cnt_ref, carry_ref):
    i = pl.program_id(0)

    @pl.when(i == 0)
    def _():
        carry_ref[...] = jnp.zeros_like(carry_ref)

    h = _rms(x_ref[...], g_ref[...]) * (1 + sc_ref[...]) + sh_ref[...]
    h_ref[...] = h
    hi = h.astype(BF16)
    lo = (h - hi.astype(F32)).astype(BF16)
    logits = _dot(hi, whi_ref[...]) + _dot(lo, whi_ref[...]) + _dot(hi, wlo_ref[...])
    lane = lax.broadcasted_iota(jnp.int32, logits.shape, 1)
    lg = jnp.where(lane < N_EXPERTS, logits, -jnp.inf)
    v1 = jnp.max(lg, axis=-1, keepdims=True)
    i1 = jnp.min(jnp.where(lg == v1, lane, LANES), axis=-1, keepdims=True)
    lg2 = jnp.where(lane == i1, -jnp.inf, lg)
    v2 = jnp.max(lg2, axis=-1, keepdims=True)
    i2 = jnp.min(jnp.where(lg2 == v2, lane, LANES), axis=-1, keepdims=True)
    e2 = jnp.exp(v2 - v1)
    tot = 1.0 + e2
    wt_ref[...] = jnp.where(lane == 0, 1.0 / tot, jnp.where(lane == 1, e2 / tot, 0.0))
    oh1 = (lane == i1).astype(F32)
    oh2 = (lane == i2).astype(F32)
    r = lax.broadcasted_iota(jnp.int32, (TR, TR), 0)
    c = lax.broadcasted_iota(jnp.int32, (TR, TR), 1)
    earlier = (r > c).astype(BF16)
    base = carry_ref[0:1, :]
    c1 = jnp.sum(oh1, axis=0, keepdims=True)
    c2 = jnp.sum(oh2, axis=0, keepdims=True)
    r1 = jnp.sum(oh1 * (base + _dot(earlier, oh1.astype(BF16))), axis=-1, keepdims=True)
    r2 = jnp.sum(oh2 * (base + c1 + _dot(earlier, oh2.astype(BF16))), axis=-1, keepdims=True)
    total = jnp.broadcast_to(base + c1 + c2, carry_ref.shape)
    carry_ref[...] = total
    cnt_ref[...] = total
    meta_ref[...] = jnp.where(lane == 0, i1, jnp.where(lane == 1, i2, jnp.where(
        lane == 2, r1.astype(jnp.int32), jnp.where(lane == 3, r2.astype(jnp.int32), 0))))


def _norm_router(x, g, mods, w_hi, w_lo):
    rows = lambda c: pl.BlockSpec((TR, c), lambda i: (i, 0))
    full = lambda shape: pl.BlockSpec(shape, lambda i: (0, 0))
    mod = lambda k: pl.BlockSpec((None, 1, D_MODEL), lambda i: (_cond_row(i, TR), 0, k))
    return pl.pallas_call(
        _norm_router_kernel,
        grid=(T // TR,),
        in_specs=[rows(D_MODEL), full((1, D_MODEL)), mod(3), mod(4), full((D_MODEL, LANES)), full((D_MODEL, LANES))],
        out_specs=[rows(D_MODEL), rows(LANES), rows(LANES), full((8, LANES))],
        out_shape=[jax.ShapeDtypeStruct((T, D_MODEL), F32),
                   jax.ShapeDtypeStruct((T, LANES), jnp.int32),
                   jax.ShapeDtypeStruct((T, LANES), F32),
                   jax.ShapeDtypeStruct((8, LANES), F32)],
        scratch_shapes=[pltpu.VMEM((8, LANES), F32)],
        compiler_params=_cparams(("arbitrary",)),
        name="norm_router",
    )(x, g.reshape(1, D_MODEL), mods, mods, w_hi, w_lo)


def _dispatch(meta, counts):
    experts = jnp.arange(N_EXPERTS, dtype=jnp.int32)
    cnt = counts[0, :N_EXPERTS].astype(jnp.int32)
    padded = ((cnt + TM_FFN - 1) // TM_FFN) * TM_FFN
    g_end = jnp.sum(jnp.where(experts[None, :] <= experts[:, None], padded[None, :], 0), axis=1)
    g_start = g_end - padded
    e = meta[:, :TOP_K]
    start_of = jnp.sum(jnp.where(e[:, :, None] == experts[None, None, :], g_start[None, None, :], 0), axis=-1)
    slot = start_of + meta[:, TOP_K:2 * TOP_K]
    end_tiles = g_end // TM_FFN
    n_used = end_tiles[N_EXPERTS - 1]
    tiles = jnp.arange(MOE_TILES, dtype=jnp.int32)
    owner = lambda t: jnp.minimum(jnp.sum((end_tiles[None, :] <= t[:, None]).astype(jnp.int32), axis=1), N_EXPERTS - 1)
    tile_e = owner(jnp.minimum(tiles, n_used - 1))
    return slot, g_end, padded, tile_e, n_used.reshape(1)


def _row_copy(src, src_row, dst, dst_row, sem):
    return pltpu.make_async_copy(src.at[pl.ds(src_row, 1), :], dst.at[pl.ds(dst_row, 1), :], sem)


def _moe_scatter_kernel(ge_ref, pd_ref, slot_hbm, h_ref, xs_hbm, slot_smem, zero_ref, sem_ids, sem_rows, sem_zero):
    i = pl.program_id(0)

    def clear_copies(first_row):
        base = pl.multiple_of(first_row, ZR)
        return [pltpu.make_async_copy(zero_ref, xs_hbm.at[pl.ds(base + k * ZR, ZR), :], sem_zero)
                for k in range(TM_FFN // ZR)]

    def for_unused_tiles(fn):
        def body(t, carry):
            for cp in clear_copies(t * TM_FFN):
                fn(cp)
            return carry
        lax.fori_loop(ge_ref[N_EXPERTS - 1] // TM_FFN, MOE_TILES, body, 0)

    @pl.when(i == 0)
    def _():
        zero_ref[...] = jnp.zeros_like(zero_ref)
        for e in range(N_EXPERTS):
            @pl.when(pd_ref[e] > 0)
            def _():
                for cp in clear_copies(ge_ref[e] - TM_FFN):
                    cp.start()
        for_unused_tiles(lambda cp: cp.start())
        for e in range(N_EXPERTS):
            @pl.when(pd_ref[e] > 0)
            def _():
                for cp in clear_copies(ge_ref[e] - TM_FFN):
                    cp.wait()
        for_unused_tiles(lambda cp: cp.wait())

    ids_copy = pltpu.make_async_copy(slot_hbm.at[i], slot_smem, sem_ids)
    ids_copy.start()
    ids_copy.wait()

    def start(r, carry):
        _row_copy(h_ref, r, xs_hbm, slot_smem[TOP_K * r], sem_rows).start()
        _row_copy(h_ref, r, xs_hbm, slot_smem[TOP_K * r + 1], sem_rows).start()
        return carry

    lax.fori_loop(0, TR, start, 0)

    def wait(r, carry):
        _row_copy(h_ref, r, xs_hbm, 0, sem_rows).wait()
        _row_copy(h_ref, r, xs_hbm, 0, sem_rows).wait()
        return carry

    lax.fori_loop(0, TR, wait, 0)


def _moe_scatter(h, slot, g_end, padded):
    grid_spec = pltpu.PrefetchScalarGridSpec(
        num_scalar_prefetch=2,
        grid=(T // TR,),
        in_specs=[pl.BlockSpec(memory_space=pl.ANY),
                  pl.BlockSpec((TR, D_MODEL), lambda i, ge, pd: (i, 0))],
        out_specs=pl.BlockSpec(memory_space=pl.ANY),
        scratch_shapes=[pltpu.SMEM((TOP_K * TR,), jnp.int32),
                        pltpu.VMEM((ZR, D_MODEL), F32),
                        pltpu.SemaphoreType.DMA,
                        pltpu.SemaphoreType.DMA,
                        pltpu.SemaphoreType.DMA],
    )
    return pl.pallas_call(
        _moe_scatter_kernel,
        grid_spec=grid_spec,
        out_shape=jax.ShapeDtypeStruct((MOE_TILES * TM_FFN, D_MODEL), F32),
        compiler_params=_cparams(("arbitrary",)),
        name="moe_scatter",
    )(g_end, padded, slot.reshape(T // TR, TOP_K * TR), h)


def _ffn_moe_kernel(te_ref, nu_ref, xs_hbm, wg_ref, wu_ref, wd_ref, o_ref, xg_ref, xb_ref, sem):
    i = pl.program_id(0)
    j = pl.program_id(1)
    n_used = nu_ref[0]
    used = i < n_used

    def tile_copy(t):
        return pltpu.make_async_copy(xs_hbm.at[pl.ds(pl.multiple_of(t * TM_FFN, TM_FFN), TM_FFN), :], xg_ref, sem)

    @pl.when(used & (j == 0))
    def _():
        @pl.when(i == 0)
        def _():
            tile_copy(0).start()

        tile_copy(i).wait()
        xb_ref[...] = xg_ref[...].astype(BF16)

        @pl.when(i + 1 < n_used)
        def _():
            tile_copy(i + 1).start()

    @pl.when(used)
    def _():
        _accumulate(o_ref, _swiglu_step(xb_ref[...], wg_ref, wu_ref, wd_ref), j)

    @pl.when(jnp.logical_not(used) & (j == 0))
    def _():
        o_ref[...] = jnp.zeros_like(o_ref)


def _ffn_moe(xs, tile_e, n_used, wg, wu, wd):
    def hid(i, j, te, nu):
        return jnp.where(i < nu[0], j, NF - 1)

    grid_spec = pltpu.PrefetchScalarGridSpec(
        num_scalar_prefetch=2,
        grid=(MOE_TILES, NF),
        in_specs=[pl.BlockSpec(memory_space=pl.ANY),
                  pl.BlockSpec((None, D_MODEL, TF), lambda i, j, te, nu: (te[i], 0, hid(i, j, te, nu))),
                  pl.BlockSpec((None, D_MODEL, TF), lambda i, j, te, nu: (te[i], 0, hid(i, j, te, nu))),
                  pl.BlockSpec((None, TF, D_MODEL), lambda i, j, te, nu: (te[i], hid(i, j, te, nu), 0))],
        out_specs=pl.BlockSpec((TM_FFN, D_MODEL), lambda i, j, te, nu: (i, 0)),
        scratch_shapes=[pltpu.VMEM((TM_FFN, D_MODEL), F32),
                        pltpu.VMEM((TM_FFN, D_MODEL), BF16),
                        pltpu.SemaphoreType.DMA],
    )
    return pl.pallas_call(
        _ffn_moe_kernel,
        grid_spec=grid_spec,
        out_shape=jax.ShapeDtypeStruct((MOE_TILES * TM_FFN, D_MODEL), F32),
        compiler_params=_cparams(("arbitrary", "arbitrary")),
        name="ffn_moe",
    )(tile_e, n_used, xs, wg, wu, wd)


def _combine_kernel(slot_hbm, ys_hbm, wt_ref, x_ref, gate_ref, g_ref, yp_ref, ysm_ref,
                    slot_smem, y0_ref, y1_ref, sem_ids, sem_rows, *, n_prompt_tiles):
    i = pl.program_id(0)

    def fetch(t):
        b = t % 2
        ids_copy = pltpu.make_async_copy(slot_hbm.at[t], slot_smem.at[b], sem_ids)
        ids_copy.start()
        ids_copy.wait()

        def start(r, carry):
            _row_copy(ys_hbm, slot_smem[b, TOP_K * r], y0_ref.at[b], r, sem_rows.at[b]).start()
            _row_copy(ys_hbm, slot_smem[b, TOP_K * r + 1], y1_ref.at[b], r, sem_rows.at[b]).start()
            return carry

        lax.fori_loop(0, TC, start, 0)

    @pl.when(i == 0)
    def _():
        fetch(0)

    @pl.when(i + 1 < pl.num_programs(0))
    def _():
        fetch(i + 1)

    b = i % 2

    def wait(r, carry):
        _row_copy(ys_hbm, 0, y0_ref.at[b], r, sem_rows.at[b]).wait()
        _row_copy(ys_hbm, 0, y1_ref.at[b], r, sem_rows.at[b]).wait()
        return carry

    lax.fori_loop(0, TC, wait, 0)
    wt = wt_ref[...]
    y = wt[:, 0:1] * y0_ref[b] + wt[:, 1:2] * y1_ref[b]
    out = _rms(x_ref[...] + gate_ref[...] * y, g_ref[...])

    @pl.when(i < n_prompt_tiles)
    def _():
        yp_ref[...] = out

    @pl.when(i >= n_prompt_tiles)
    def _():
        ysm_ref[...] = out


def _combine(slot, ys, wt, x, mods, k_gate, g):
    n_p = TP // TC
    return pl.pallas_call(
        functools.partial(_combine_kernel, n_prompt_tiles=n_p),
        grid=(T // TC,),
        in_specs=[pl.BlockSpec(memory_space=pl.ANY),
                  pl.BlockSpec(memory_space=pl.ANY),
                  pl.BlockSpec((TC, LANES), lambda i: (i, 0)),
                  pl.BlockSpec((TC, D_MODEL), lambda i: (i, 0)),
                  pl.BlockSpec((None, 1, D_MODEL), lambda i: (_cond_row(i, TC), 0, k_gate)),
                  pl.BlockSpec((1, D_MODEL), lambda i: (0, 0))],
        out_specs=[pl.BlockSpec((TC, D_MODEL), lambda i: (jnp.minimum(i, n_p - 1), 0)),
                   pl.BlockSpec((TC, D_MODEL), lambda i: (jnp.maximum(i - n_p, 0), 0))],
        out_shape=[jax.ShapeDtypeStruct((TP, D_MODEL), F32), jax.ShapeDtypeStruct((TS, D_MODEL), F32)],
        scratch_shapes=[pltpu.SMEM((2, TOP_K * TC), jnp.int32),
                        pltpu.VMEM((2, TC, D_MODEL), F32),
                        pltpu.VMEM((2, TC, D_MODEL), F32),
                        pltpu.SemaphoreType.DMA,
                        pltpu.SemaphoreType.DMA((2,))],
        compiler_params=_cparams(("arbitrary",)),
        name="moe_combine",
    )(slot.reshape(T // TC, TOP_K * TC), ys, wt, x, mods, g.reshape(1, D_MODEL))


def kernel(x_prompt, x_sample, cache_attn_k, cache_attn_v, cache_mla_ckv, cache_mla_krope, c, c_ctx, ln1_g, ln2_g, w_ada, b_ada, w_in, attn_sink, mla_q_norm_g, w_uq, mla_kv_norm_g, w_ukv, pool_w, pool_scale, w_branch_a, w_branch_b, w_branch_c, w_out, ffn_w_gate, ffn_w_up, ffn_w_down, router_w, moe_w_gate, moe_w_up, moe_w_down, final_g):
    x = jnp.concatenate([x_prompt.reshape(TP, D_MODEL), x_sample.reshape(TS, D_MODEL)], axis=0)
    cond = jnp.concatenate([c_ctx[None, :], c, jnp.zeros((N_COND - 1 - DEC_BATCH, D_MODEL), F32)], axis=0)
    rows = DEC_SEQ // GRID_W
    cos_a, sin_a = _rope_tables(rows, HD_A)
    cos_b, sin_b = _rope_tables(rows, QK_ROPE)
    st_k, st_v, st_ckv, st_kr = [], [], [], []
    assert DEPTH == 2, "layer 0 is the dense-FFN layer, layer 1 the expert layer that ends the trunk"
    all_mods = [_adaln(cond, w_ada, b_ada, l).reshape(N_COND, 1, 6 * D_MODEL) for l in range(DEPTH)]
    cos_t = jnp.concatenate([jnp.ones((TM, LANES), F32), cos_b], axis=0)
    sin_t = jnp.concatenate([jnp.zeros((TM, LANES), F32), sin_b], axis=0)
    h1 = _norm_mod(x, ln1_g[0], all_mods[0], 0, 1, BF16, "norm_mod1")
    for l in range(DEPTH):
        mods = all_mods[l]
        qa = _matmul_wcast(h1, w_in, F32, A_W, "proj_q", layer=l, col0=0, n=A_W)
        kv = _matmul_wcast(h1, w_in, F32, 2 * KV_W, "proj_kv", layer=l, col0=OFF_K, n=2 * KV_W)
        zb = _matmul_wcast(h1, w_in[l][:, OFF_CQ:OFF_POOL], F32, OFF_POOL - OFF_CQ, "proj_mla")
        u = _matmul_wcast(h1, w_in[l][:, OFF_POOL:OFF_GATE], F32, C_W, "proj_pool")
        gates = _matmul_wcast(h1, w_in[l][:, OFF_GATE:], BF16, 1024, "proj_gates")
        sink = attn_sink[l]
        ck = cache_attn_k[:, l].reshape(DEC_BATCH, PAST_LEN, KV_W)
        cv = cache_attn_v[:, l].reshape(DEC_BATCH, PAST_LEN, KV_W)
        oa = (_attn_a_prompt(qa, kv, sink), _attn_a_sample(qa, kv, ck, cv, sink, cos_a, sin_a))
        wq_cat, w_aug = _mla_weights(w_uq[l], w_ukv[l])
        q, ckv, kr, xa = _mla_prep(zb, mla_q_norm_g[l], mla_kv_norm_g[l], wq_cat, cos_t, sin_t)
        cache_aug = jnp.concatenate([cache_mla_ckv[:, l], cache_mla_krope[:, l]], axis=-1).astype(BF16)
        xa_s = jnp.concatenate([xa[TP:].reshape(DEC_BATCH, DEC_SEQ, MLA_AUG), cache_aug], axis=1)
        kv_p = _matmul(xa[:TP], w_aug, BF16, TM, B_W, "mla_kv_prompt")
        kv_s = _matmul(xa_s.reshape(DEC_BATCH * MLA_KEYS, MLA_AUG), w_aug, BF16, TM, B_W, "mla_kv_sample")
        ob = (_mla_attend(q, kv_p, SEQ, 0, BATCH, 1, SEQ, "mla_prompt"),
              _mla_attend(q, kv_s, TQ_MLA, TP // TQ_MLA, DEC_BATCH, DEC_SEQ // TQ_MLA, MLA_KEYS, "mla_sample"))
        pw = pool_w[l].astype(BF16)
        oc = (_pool(u, pw, pool_scale[l], SEQ, BATCH, 0, "pool_prompt"),
              _pool(u, pw, pool_scale[l], DEC_SEQ, DEC_BATCH, TP // DEC_SEQ, "pool_sample"))
        y = _merge((oa[0], ob[0], oc[0]), (oa[1], ob[1], oc[1]), gates, w_branch_a[l].astype(BF16),
                   w_branch_b[l].astype(BF16), w_branch_c[l].astype(BF16))
        st_k.append(kv[:TP, :KV_W].reshape(BATCH, SEQ, N_KV_A, HD_A))
        st_v.append(kv[:TP, KV_W:].reshape(BATCH, SEQ, N_KV_A, HD_A))
        st_ckv.append(ckv[:TP].reshape(BATCH, SEQ, KV_LORA))
        st_kr.append(kr[:TP].reshape(BATCH, SEQ, QK_ROPE))
        if l == 0:
            x, h2 = _out_proj(y, w_out[l].astype(BF16), x, mods, ln2_g[l], True)
            f = _ffn_dense(h2, ffn_w_gate[0].astype(BF16), ffn_w_up[0].astype(BF16), ffn_w_down[0].astype(BF16))
            x, h1 = _resid_norm(x, f, mods, 5, ln1_g[1], all_mods[1], 0, 1)
        else:
            (x,) = _out_proj(y, w_out[l].astype(BF16), x, mods, ln2_g[l], False)
            rw = jnp.pad(router_w[0], ((0, 0), (0, LANES - N_EXPERTS)))
            rw_hi = rw.astype(BF16)
            rw_lo = (rw - rw_hi.astype(F32)).astype(BF16)
            h2, meta, wt, counts = _norm_router(x, ln2_g[l], mods, rw_hi, rw_lo)
            slot, g_end, padded, tile_e, n_used = _dispatch(meta, counts)
            xs = _moe_scatter(h2, slot, g_end, padded)
            ys = _ffn_moe(xs, tile_e, n_used, moe_w_gate[0].astype(BF16), moe_w_up[0].astype(BF16),
                          moe_w_down[0].astype(BF16))
            y_p, y_s = _combine(slot, ys, wt, x, mods, 5, final_g)
    return (y_p.reshape(BATCH, SEQ, D_MODEL), y_s.reshape(DEC_BATCH, DEC_SEQ, D_MODEL),
            jnp.stack(st_k, axis=1), jnp.stack(st_v, axis=1), jnp.stack(st_ckv, axis=1), jnp.stack(st_kr, axis=1))
```

```python
import functools
import math

import jax
import jax.numpy as jnp
from jax import lax
from jax.experimental import pallas as pl
from jax.experimental.pallas import tpu as pltpu

BF16 = jnp.bfloat16
F32 = jnp.float32

D_MODEL = 2048
BATCH = 16
SEQ = 256
DEPTH = 2
DEC_BATCH = 8
DEC_SEQ = 2048
PAST_LEN = 512
GRID_W = 64
BLK = 128
EPS = 1e-6
ROPE_BASE = 10000.0
NEG_INF = -1e30
N_HEADS_A = 16
N_KV_A = 2
HD_A = 64
WINDOW = 128
A_W = N_HEADS_A * HD_A
KV_W = N_KV_A * HD_A
N_HEADS_B = 8
Q_LORA = 512
KV_LORA = 256
QK_NOPE = 128
QK_ROPE = 64
V_HD = 128
B_W = N_HEADS_B * V_HD
POOL_WINDOWS = (2, 4, 8, 16)
POOL_GROUPS = 4
POOL_GW = 256
C_W = POOL_GROUPS * POOL_GW
OFF_K = A_W
OFF_V = OFF_K + KV_W
OFF_CQ = OFF_V + KV_W
OFF_CKV = OFF_CQ + Q_LORA
OFF_KR = OFF_CKV + KV_LORA
OFF_POOL = OFF_KR + QK_ROPE
OFF_GATE = OFF_POOL + C_W
IN_COLS = OFF_GATE + 3 * D_MODEL
D_FF = 5632
N_EXPERTS = 8
TOP_K = 2

TP = BATCH * SEQ
TS = DEC_BATCH * DEC_SEQ
T = TP + TS
N_COND = 16
MLA_KEYS = DEC_SEQ + PAST_LEN
LANES = 128
POOL_HALO = 8
LOG2E = math.log2(math.e)
MLA_HEAD_K = 256
MLA_AUG = KV_LORA + QK_ROPE

VMEM_LIMIT = 56 * 1024 * 1024

TM = 1024
TM_FFN = 1024
TF = 512
NF = D_FF // TF
TF_MOE = 256
MOE_TILES = (TOP_K * T) // TM_FFN + N_EXPERTS
TR = 512
ZR = 256
TC = 256
TQ_MLA = 512


def _cparams(sem):
    return pltpu.CompilerParams(dimension_semantics=sem, vmem_limit_bytes=VMEM_LIMIT)


def _cond_row(i, tm):
    n_p = TP // tm
    per_b = DEC_SEQ // tm
    return jnp.where(i < n_p, 0, (i - n_p) // per_b + 1)


def _dot(a, b):
    return jnp.dot(a, b, preferred_element_type=F32)


def _dot_nt(a, b):
    return lax.dot_general(a, b, (((1,), (1,)), ((), ())), preferred_element_type=F32)


def _mm_kernel(x_ref, w_ref, o_ref):
    o_ref[...] = _dot(x_ref[...].astype(BF16), w_ref[...].astype(BF16)).astype(o_ref.dtype)


def _matmul(x, w, out_dtype, tm, tn, name):
    m, k = x.shape
    n = w.shape[1]
    return pl.pallas_call(
        _mm_kernel,
        grid=(m // tm, n // tn),
        in_specs=[pl.BlockSpec((tm, k), lambda i, j: (i, 0)),
                  pl.BlockSpec((k, tn), lambda i, j: (0, j))],
        out_specs=pl.BlockSpec((tm, tn), lambda i, j: (i, j)),
        out_shape=jax.ShapeDtypeStruct((m, n), out_dtype),
        compiler_params=_cparams(("parallel", "arbitrary")),
        name=name,
    )(x, w)


def _mm_wt_kernel(x_ref, wt_hbm, o_ref, stage_ref, wb_ref, sem, *, layer, row0, tn):
    j = pl.program_id(0)

    @pl.when(pl.program_id(1) == 0)
    def _():
        first = pl.multiple_of(row0 + j * tn, 8)
        copy = pltpu.make_async_copy(wt_hbm.at[layer, pl.ds(first, tn), :], stage_ref, sem)
        copy.start()
        copy.wait()
        wb_ref[...] = stage_ref[...].astype(BF16)

    o_ref[...] = _dot_nt(x_ref[...], wb_ref[...]).astype(o_ref.dtype)


def _matmul_wt(x, wt, layer, row0, n, tn, out_dtype, name):
    m, k = x.shape
    tm = TM
    assert n % tn == 0 and row0 % 8 == 0 and tn % 16 == 0
    return pl.pallas_call(
        functools.partial(_mm_wt_kernel, layer=layer, row0=row0, tn=tn),
        grid=(n // tn, m // tm),
        in_specs=[pl.BlockSpec((tm, k), lambda j, i: (i, 0)), pl.BlockSpec(memory_space=pl.ANY)],
        out_specs=pl.BlockSpec((tm, tn), lambda j, i: (i, j)),
        out_shape=jax.ShapeDtypeStruct((m, n), out_dtype),
        scratch_shapes=[pltpu.VMEM((tn, k), F32), pltpu.VMEM((tn, k), BF16), pltpu.SemaphoreType.DMA],
        compiler_params=_cparams(("parallel", "arbitrary")),
        name=name,
    )(x, wt)


def _ada_kernel(c_ref, w_ref, b_ref, o_ref):
    c = c_ref[...]
    a = (c * jax.nn.sigmoid(c)).astype(BF16)
    o_ref[...] = _dot(a, w_ref[...].astype(BF16)) + b_ref[...]


def _adaln(cond, w, b, layer):
    n = w.shape[2]
    tn = 1024
    return pl.pallas_call(
        _ada_kernel,
        grid=(n // tn,),
        in_specs=[pl.BlockSpec((N_COND, D_MODEL), lambda j: (0, 0)),
                  pl.BlockSpec((None, D_MODEL, tn), lambda j: (layer, 0, j)),
                  pl.BlockSpec((None, 1, tn), lambda j: (layer, 0, j))],
        out_specs=pl.BlockSpec((N_COND, tn), lambda j: (0, j)),
        out_shape=jax.ShapeDtypeStruct((N_COND, n), F32),
        compiler_params=_cparams(("arbitrary",)),
        name="adaln",
    )(cond, w, b.reshape(DEPTH, 1, n))


def _split_row_specs(tm, width):
    n_p = TP // tm
    return [pl.BlockSpec((tm, width), lambda i, *_: (jnp.minimum(i, n_p - 1), 0)),
            pl.BlockSpec((tm, width), lambda i, *_: (jnp.maximum(i - n_p, 0), 0))], n_p


def _pick_rows(p_ref, s_ref, n_prompt_tiles):
    return jnp.where(pl.program_id(0) < n_prompt_tiles, p_ref[...], s_ref[...])


def _norm_mod_kernel(xp_ref, xs_ref, g_ref, sh_ref, sc_ref, o_ref, *, n_prompt_tiles):
    x = _pick_rows(xp_ref, xs_ref, n_prompt_tiles)
    o_ref[...] = (_rms(x, g_ref[...]) * (1 + sc_ref[...]) + sh_ref[...]).astype(o_ref.dtype)


def _norm_mod(xp, xs, g, mods, k_shift, k_scale):
    tm = 512
    x_specs, n_p = _split_row_specs(tm, D_MODEL)
    return pl.pallas_call(
        functools.partial(_norm_mod_kernel, n_prompt_tiles=n_p),
        grid=(T // tm,),
        in_specs=x_specs + [pl.BlockSpec((1, D_MODEL), lambda i: (0, 0)),
                            pl.BlockSpec((None, 1, D_MODEL), lambda i: (_cond_row(i, tm), 0, k_shift)),
                            pl.BlockSpec((None, 1, D_MODEL), lambda i: (_cond_row(i, tm), 0, k_scale))],
        out_specs=pl.BlockSpec((tm, D_MODEL), lambda i: (i, 0)),
        out_shape=jax.ShapeDtypeStruct((T, D_MODEL), BF16),
        compiler_params=_cparams(("parallel",)),
        name="norm_mod1",
    )(xp, xs, g.reshape(1, D_MODEL), mods, mods)


def _rms(x, g):
    return x * lax.rsqrt(jnp.mean(x * x, axis=-1, keepdims=True) + EPS) * g


def _resid_norm_kernel(x_ref, f_ref, gate_ref, g_ref, sh_ref, sc_ref, x2_ref, h_ref):
    x2 = x_ref[...] + gate_ref[...] * f_ref[...]
    x2_ref[...] = x2
    h_ref[...] = (_rms(x2, g_ref[...]) * (1 + sc_ref[...]) + sh_ref[...]).astype(h_ref.dtype)


def _resid_norm(x, f, mods, k_gate, g, mods_next, k_shift, k_scale):
    tm = 512
    rows = pl.BlockSpec((tm, D_MODEL), lambda i: (i, 0))
    mod = lambda k: pl.BlockSpec((None, 1, D_MODEL), lambda i: (_cond_row(i, tm), 0, k))
    return pl.pallas_call(
        _resid_norm_kernel,
        grid=(T // tm,),
        in_specs=[rows, rows, mod(k_gate), pl.BlockSpec((1, D_MODEL), lambda i: (0, 0)), mod(k_shift), mod(k_scale)],
        out_specs=[rows, rows],
        out_shape=[jax.ShapeDtypeStruct((T, D_MODEL), F32), jax.ShapeDtypeStruct((T, D_MODEL), BF16)],
        compiler_params=_cparams(("parallel",)),
        name="resid_norm",
    )(x, f, mods, g.reshape(1, D_MODEL), mods_next, mods_next)


def _rope_tables(rows, dim):
    quarter = dim // 4
    inv = ROPE_BASE ** (-jnp.arange(quarter, dtype=F32) / quarter)
    r = jnp.repeat(jnp.arange(rows, dtype=F32), GRID_W)
    col = jnp.tile(jnp.arange(GRID_W, dtype=F32), rows)
    ar, ac = r[:, None] * inv, col[:, None] * inv
    cos = jnp.concatenate([jnp.cos(ar), jnp.cos(ar), jnp.cos(ac), jnp.cos(ac)], axis=-1)
    sin = jnp.concatenate([-jnp.sin(ar), jnp.sin(ar), -jnp.sin(ac), jnp.sin(ac)], axis=-1)
    reps = LANES // dim
    return jnp.tile(cos, (1, reps)), jnp.tile(sin, (1, reps))


def _rope_lanes(x, cos, sin, quarter):
    lane = lax.broadcasted_iota(jnp.int32, (x.shape[0], LANES), 1)
    first = (lane % (2 * quarter)) < quarter
    outs = []
    for c in range(x.shape[1] // LANES):
        xc = x[:, c * LANES:(c + 1) * LANES]
        partner = jnp.where(first, pltpu.roll(xc, LANES - quarter, 1), pltpu.roll(xc, quarter, 1))
        outs.append(xc * cos + partner * sin)
    return outs[0] if len(outs) == 1 else jnp.concatenate(outs, axis=1)


def _attn_a_kernel(*refs, local, tq, n_blocks):
    assert N_KV_A * HD_A == LANES
    if local:
        (sink_ref, q_ref, kc_ref, vc_ref, kvm_ref, kv0_ref, kvp_ref,
         cq_ref, sq_ref, ckm_ref, skm_ref, ck0_ref, sk0_ref, ckp_ref, skp_ref, o_ref) = refs
    else:
        sink_ref, q_ref, kc_ref, vc_ref, o_ref = refs
    quarter = HD_A // 4
    q = q_ref[...]
    k_all = kc_ref[...]
    v_all = vc_ref[...]
    n_loc = 0
    if local:
        j = pl.program_id(1)
        n_loc = 3 * BLK
        q = _rope_lanes(q, cq_ref[...], sq_ref[...], quarter)
        k_all = jnp.concatenate([
            _rope_lanes(kvm_ref[:, :KV_W], ckm_ref[...], skm_ref[...], quarter),
            _rope_lanes(kv0_ref[:, :KV_W], ck0_ref[...], sk0_ref[...], quarter),
            _rope_lanes(kvp_ref[:, :KV_W], ckp_ref[...], skp_ref[...], quarter), k_all], axis=0)
        v_all = jnp.concatenate([kvm_ref[:, KV_W:], kv0_ref[:, KV_W:], kvp_ref[:, KV_W:], v_all], axis=0)
        qi = lax.broadcasted_iota(jnp.int32, (tq, n_loc), 0)
        ki = lax.broadcasted_iota(jnp.int32, (tq, n_loc), 1)
        k_lo = jnp.where(j == 0, BLK, 0)
        k_hi = jnp.where(j == n_blocks - 1, 2 * BLK, 3 * BLK)
        valid = (ki >= qi) & (ki <= qi + 2 * WINDOW) & (ki >= k_lo) & (ki < k_hi)
        bias = jnp.where(valid, 0.0, NEG_INF)
    q = (q * (HD_A ** -0.5 * LOG2E)).astype(BF16)
    nk = k_all.shape[0]
    low_k = lax.broadcasted_iota(jnp.int32, (nk, LANES), 1) < HD_A
    low_q = lax.broadcasted_iota(jnp.int32, (tq, LANES), 1) < HD_A
    rep = N_HEADS_A // N_KV_A

    def blockdiag(x, g):
        swapped = pltpu.roll(x, HD_A, 1)
        lo_src, hi_src = (x, swapped) if g == 0 else (swapped, x)
        return jnp.concatenate([jnp.where(low_k, lo_src, 0.0), jnp.where(low_k, 0.0, hi_src)], axis=0).astype(BF16)

    for g in range(N_KV_A):
        k2 = blockdiag(k_all, g)
        v2 = blockdiag(v_all, g)
        for p in range(rep // 2):
            h0 = g * rep + 2 * p
            s = _dot_nt(q[:, h0 * HD_A:h0 * HD_A + LANES], k2)
            parts, inv = [], []
            for t in range(2):
                sh = s[:, t * nk:(t + 1) * nk]
                sk = sink_ref[h0 + t] * LOG2E
                if local:
                    s_loc = sh[:, :n_loc] + bias
                    s_ctx = sh[:, n_loc:]
                    m = jnp.maximum(jnp.maximum(jnp.max(s_loc, axis=-1, keepdims=True),
                                                jnp.max(s_ctx, axis=-1, keepdims=True)), sk)
                    p_loc = jnp.exp2(s_loc - m)
                    p_ctx = jnp.exp2(s_ctx - m)
                    denom = (jnp.sum(p_loc, axis=-1, keepdims=True) + jnp.sum(p_ctx, axis=-1, keepdims=True)
                             + jnp.exp2(sk - m))
                    parts += [p_loc, p_ctx]
                else:
                    m = jnp.maximum(jnp.max(sh, axis=-1, keepdims=True), sk)
                    p_all = jnp.exp2(sh - m)
                    denom = jnp.sum(p_all, axis=-1, keepdims=True) + jnp.exp2(sk - m)
                    parts.append(p_all)
                inv.append(1.0 / denom)
            o2 = _dot(jnp.concatenate(parts, axis=1).astype(BF16), v2) * jnp.where(low_q, inv[0], inv[1])
            o_ref[:, h0 * HD_A:h0 * HD_A + LANES] = o2.astype(o_ref.dtype)


def _attn_a_prompt(qa, kv, sink):
    kern = functools.partial(_attn_a_kernel, local=False, tq=SEQ, n_blocks=1)
    return pl.pallas_call(
        kern,
        grid=(BATCH,),
        in_specs=[pl.BlockSpec(memory_space=pltpu.SMEM),
                  pl.BlockSpec((SEQ, A_W), lambda b: (b, 0)),
                  pl.BlockSpec((SEQ, KV_W), lambda b: (b, 0)),
                  pl.BlockSpec((SEQ, KV_W), lambda b: (b, 1))],
        out_specs=pl.BlockSpec((SEQ, A_W), lambda b: (b, 0)),
        out_shape=jax.ShapeDtypeStruct((TP, A_W), BF16),
        compiler_params=_cparams(("parallel",)),
        name="attn_a_prompt",
    )(sink, qa, kv, kv)


def _attn_a_sample(qa, kv, ck, cv, sink, cos, sin):
    nb = DEC_SEQ // BLK
    base = TP // BLK

    def row(b, j):
        return base + b * nb + j

    def tab_prev(b, j):
        return (jnp.maximum(j - 1, 0), 0)

    def tab_next(b, j):
        return (jnp.minimum(j + 1, nb - 1), 0)

    kv_spec = lambda f: pl.BlockSpec((BLK, 2 * KV_W), f)
    tab_spec = lambda f: pl.BlockSpec((BLK, LANES), f)
    kern = functools.partial(_attn_a_kernel, local=True, tq=BLK, n_blocks=nb)
    return pl.pallas_call(
        kern,
        grid=(DEC_BATCH, nb),
        in_specs=[pl.BlockSpec(memory_space=pltpu.SMEM),
                  pl.BlockSpec((BLK, A_W), lambda b, j: (row(b, j), 0)),
                  pl.BlockSpec((None, PAST_LEN, KV_W), lambda b, j: (b, 0, 0)),
                  pl.BlockSpec((None, PAST_LEN, KV_W), lambda b, j: (b, 0, 0)),
                  kv_spec(lambda b, j: (row(b, jnp.maximum(j - 1, 0)), 0)),
                  kv_spec(lambda b, j: (row(b, j), 0)),
                  kv_spec(lambda b, j: (row(b, jnp.minimum(j + 1, nb - 1)), 0)),
                  tab_spec(lambda b, j: (j, 0)), tab_spec(lambda b, j: (j, 0)),
                  tab_spec(tab_prev), tab_spec(tab_prev),
                  tab_spec(lambda b, j: (j, 0)), tab_spec(lambda b, j: (j, 0)),
                  tab_spec(tab_next), tab_spec(tab_next)],
        out_specs=pl.BlockSpec((BLK, A_W), lambda b, j: (b * nb + j, 0)),
        out_shape=jax.ShapeDtypeStruct((TS, A_W), BF16),
        compiler_params=_cparams(("parallel", "arbitrary")),
        name="attn_a_sample",
    )(sink, qa, ck, cv, kv, kv, kv, cos, sin, cos, sin, cos, sin, cos, sin)


def _mla_prep_kernel(z_ref, gq_ref, gkv_ref, wq_ref, cos_ref, sin_ref, q_ref, ckv_ref, kr_ref, xa_ref):
    quarter = QK_ROPE // 4
    c = (QK_NOPE + QK_ROPE) ** -0.5 * LOG2E
    cos = cos_ref[...]
    sin = sin_ref[...]
    cq = z_ref[:, :Q_LORA]
    q = _dot(_rms(cq, gq_ref[...]).astype(BF16), wq_ref[...])
    for h in range(N_HEADS_B):
        lo = h * MLA_HEAD_K
        q_ref[:, lo:lo + QK_NOPE] = (q[:, lo:lo + QK_NOPE] * c).astype(q_ref.dtype)
        rot = _rope_lanes(q[:, lo + QK_NOPE:lo + MLA_HEAD_K], cos, sin, quarter)
        q_ref[:, lo + QK_NOPE:lo + MLA_HEAD_K] = (rot * c).astype(q_ref.dtype)
    ckv = _rms(z_ref[:, Q_LORA:Q_LORA + KV_LORA], gkv_ref[...])
    ckv_ref[...] = ckv
    kr = z_ref[:, Q_LORA + KV_LORA:]
    kr_ref[...] = kr
    xa_ref[:, :KV_LORA] = ckv.astype(xa_ref.dtype)
    kr_rot = _rope_lanes(jnp.concatenate([kr, kr], axis=1), cos, sin, quarter)
    xa_ref[:, KV_LORA:] = kr_rot[:, :QK_ROPE].astype(xa_ref.dtype)


def _mla_prep(z, gq, gkv, wq_cat, cos, sin):
    tm = TM
    zc = Q_LORA + KV_LORA + QK_ROPE
    n_p = TP // tm
    per_b = DEC_SEQ // tm
    full = lambda shape: pl.BlockSpec(shape, lambda i: (0, 0))
    rows = lambda c: pl.BlockSpec((tm, c), lambda i: (i, 0))
    tab = pl.BlockSpec((tm, LANES), lambda i: (jnp.where(i < n_p, 0, 1 + (i - n_p) % per_b), 0))
    return pl.pallas_call(
        _mla_prep_kernel,
        grid=(T // tm,),
        in_specs=[rows(zc), full((1, Q_LORA)), full((1, KV_LORA)), full((Q_LORA, N_HEADS_B * MLA_HEAD_K)), tab, tab],
        out_specs=[rows(N_HEADS_B * MLA_HEAD_K), rows(KV_LORA), rows(QK_ROPE), rows(MLA_AUG)],
        out_shape=[jax.ShapeDtypeStruct((T, N_HEADS_B * MLA_HEAD_K), BF16),
                   jax.ShapeDtypeStruct((T, KV_LORA), F32),
                   jax.ShapeDtypeStruct((T, QK_ROPE), F32),
                   jax.ShapeDtypeStruct((T, MLA_AUG), BF16)],
        compiler_params=_cparams(("parallel",)),
        name="mla_prep",
    )(z, gq.reshape(1, Q_LORA), gkv.reshape(1, KV_LORA), wq_cat, cos, sin)


def _mla_weights(w_uq, w_ukv):
    wq = w_uq.reshape(Q_LORA, N_HEADS_B, QK_NOPE + QK_ROPE)
    wq = jnp.pad(wq, ((0, 0), (0, 0), (0, MLA_HEAD_K - QK_NOPE - QK_ROPE))).reshape(Q_LORA, N_HEADS_B * MLA_HEAD_K)
    wkv = w_ukv.reshape(KV_LORA, N_HEADS_B, QK_NOPE + V_HD)
    wk = jnp.pad(wkv[:, :, :QK_NOPE], ((0, QK_ROPE), (0, 0), (0, MLA_HEAD_K - QK_NOPE)))
    eye = jnp.pad(jnp.eye(QK_ROPE, dtype=F32), ((KV_LORA, 0), (QK_NOPE, MLA_HEAD_K - QK_NOPE - QK_ROPE)))
    wk = (wk + eye[:, None, :]).reshape(MLA_AUG, N_HEADS_B * MLA_HEAD_K)
    wv = jnp.pad(wkv[:, :, QK_NOPE:], ((0, QK_ROPE), (0, 0), (0, 0))).reshape(MLA_AUG, B_W)
    return wq.astype(BF16), jnp.concatenate([wk, wv], axis=1).astype(BF16)


def _mla_kernel(q_ref, k_ref, v_ref, o_ref):
    for h in range(N_HEADS_B):
        ks = slice(h * MLA_HEAD_K, (h + 1) * MLA_HEAD_K)
        vs = slice(h * V_HD, (h + 1) * V_HD)
        s = _dot_nt(q_ref[:, ks], k_ref[:, ks])
        p = jnp.exp2(s - jnp.max(s, axis=-1, keepdims=True))
        denom = jnp.sum(p, axis=-1, keepdims=True)
        o_ref[:, vs] = (_dot(p.astype(BF16), v_ref[:, vs]) / denom).astype(o_ref.dtype)


def _mla_attend(q, kv, tq, q_base, n_seq, nq, nk, name):
    kw = N_HEADS_B * MLA_HEAD_K
    return pl.pallas_call(
        _mla_kernel,
        grid=(n_seq, nq),
        in_specs=[pl.BlockSpec((tq, kw), lambda b, i: (q_base + b * nq + i, 0)),
                  pl.BlockSpec((nk, kw), lambda b, i: (b, 0), pipeline_mode=pl.Buffered(1)),
                  pl.BlockSpec((nk, B_W), lambda b, i: (b, kw // B_W), pipeline_mode=pl.Buffered(1))],
        out_specs=pl.BlockSpec((tq, B_W), lambda b, i: (b * nq + i, 0)),
        out_shape=jax.ShapeDtypeStruct((n_seq * nq * tq, B_W), BF16),
        compiler_params=_cparams(("parallel", "arbitrary")),
        name=name,
    )(q, kv, kv)


def _pool_kernel(u_ref, w_ref, s_ref, o_ref, pad_ref, *, n):
    chunk = min(n, 256)
    zeros = jnp.zeros((POOL_HALO, C_W), F32)
    pad_ref[0:POOL_HALO, :] = zeros
    pad_ref[POOL_HALO + n:POOL_HALO + n + POOL_HALO, :] = zeros
    pad_ref[POOL_HALO:POOL_HALO + n, :] = u_ref[...]
    for c in range(n // chunk):
        r0 = c * chunk
        t = lax.broadcasted_iota(jnp.int32, (chunk, 1), 0) + r0
        for g, win in enumerate(POOL_WINDOWS):
            left = win // 2
            right = win - left - 1
            cols = slice(g * POOL_GW, (g + 1) * POOL_GW)
            acc = pad_ref[POOL_HALO + r0 - left:POOL_HALO + r0 - left + chunk, cols]
            for k in range(-left + 1, right + 1):
                acc = acc + pad_ref[POOL_HALO + r0 + k:POOL_HALO + r0 + k + chunk, cols]
            cnt = (jnp.minimum(t + right, n - 1) + 1 - jnp.maximum(t - left, 0)).astype(F32)
            d = acc / cnt - pad_ref[POOL_HALO + r0:POOL_HALO + r0 + chunk, cols]
            y = _dot(d.astype(BF16), w_ref[g]) * s_ref[:, cols]
            o_ref[r0:r0 + chunk, cols] = y.astype(o_ref.dtype)


def _pool(u, w, scale, n, n_seq, row_base, name):
    kern = functools.partial(_pool_kernel, n=n)
    return pl.pallas_call(
        kern,
        grid=(n_seq,),
        in_specs=[pl.BlockSpec((n, C_W), lambda b: (row_base + b, 0)),
                  pl.BlockSpec((POOL_GROUPS, POOL_GW, POOL_GW), lambda b: (0, 0, 0)),
                  pl.BlockSpec((1, C_W), lambda b: (0, 0))],
        out_specs=pl.BlockSpec((n, C_W), lambda b: (b, 0)),
        out_shape=jax.ShapeDtypeStruct((n_seq * n, C_W), BF16),
        scratch_shapes=[pltpu.VMEM((n + 2 * POOL_HALO, C_W), F32)],
        compiler_params=_cparams(("parallel",)),
        name=name,
    )(u, w, scale.reshape(1, C_W))


def _merge_kernel(oap_ref, obp_ref, ocp_ref, oas_ref, obs_ref, ocs_ref, ga_ref, gb_ref, gc_ref,
                  wa_ref, wb_ref, wc_ref, o_ref, *, n_prompt_tiles):
    is_p = pl.program_id(0) < n_prompt_tiles
    pick = lambda p_ref, s_ref: jnp.where(is_p, p_ref[...], s_ref[...])
    y = jax.nn.sigmoid(ga_ref[...].astype(F32)) * _dot(pick(oap_ref, oas_ref), wa_ref[...])
    y = y + jax.nn.sigmoid(gb_ref[...].astype(F32)) * _dot(pick(obp_ref, obs_ref), wb_ref[...])
    y = y + jax.nn.sigmoid(gc_ref[...].astype(F32)) * _dot(pick(ocp_ref, ocs_ref), wc_ref[...])
    o_ref[...] = y.astype(o_ref.dtype)


def _merge(branches_p, branches_s, gates, wa, wb, wc):
    tm, tn = TM, 1024
    nn = D_MODEL // tn
    n_p = TP // tm
    rows_p = pl.BlockSpec((tm, A_W), lambda i, j: (jnp.minimum(i, n_p - 1), 0), pipeline_mode=pl.Buffered(1))
    rows_s = pl.BlockSpec((tm, A_W), lambda i, j: (jnp.maximum(i - n_p, 0), 0))
    gate = lambda k: pl.BlockSpec((tm, tn), lambda i, j: (i, k * nn + j))
    wspec = pl.BlockSpec((A_W, tn), lambda i, j: (0, j))
    return pl.pallas_call(
        functools.partial(_merge_kernel, n_prompt_tiles=n_p),
        grid=(T // tm, nn),
        in_specs=[rows_p] * 3 + [rows_s] * 3 + [gate(0), gate(1), gate(2), wspec, wspec, wspec],
        out_specs=pl.BlockSpec((tm, tn), lambda i, j: (i, j)),
        out_shape=jax.ShapeDtypeStruct((T, D_MODEL), BF16),
        compiler_params=_cparams(("parallel", "arbitrary")),
        name="merge",
    )(*branches_p, *branches_s, gates, gates, gates, wa, wb, wc)


def _out_proj_kernel(*refs, n_x, n_prompt_tiles):
    y_ref, w_ref = refs[:2]
    x_refs = refs[2:2 + n_x]
    gate_ref, g_ref, sh_ref, sc_ref, x1_ref = refs[2 + n_x:7 + n_x]
    h_refs = refs[7 + n_x:]
    x = x_refs[0][...] if n_x == 1 else _pick_rows(x_refs[0], x_refs[1], n_prompt_tiles)
    x1 = x + gate_ref[...] * _dot(y_ref[...], w_ref[...])
    x1_ref[...] = x1
    if h_refs:
        h_refs[0][...] = (_rms(x1, g_ref[...]) * (1 + sc_ref[...]) + sh_ref[...]).astype(h_refs[0].dtype)


def _out_proj(y, w, x, mods, g, emit_h):
    tm = 512
    rows = pl.BlockSpec((tm, D_MODEL), lambda i: (i, 0))
    mod = lambda k: pl.BlockSpec((None, 1, D_MODEL), lambda i: (_cond_row(i, tm), 0, k))
    if isinstance(x, tuple):
        x_specs, n_p = _split_row_specs(tm, D_MODEL)
    else:
        x, x_specs, n_p = (x,), [rows], 0
    n_out = 2 if emit_h else 1
    return pl.pallas_call(
        functools.partial(_out_proj_kernel, n_x=len(x), n_prompt_tiles=n_p),
        grid=(T // tm,),
        in_specs=[rows, pl.BlockSpec((D_MODEL, D_MODEL), lambda i: (0, 0))] + x_specs + [
            mod(2), pl.BlockSpec((1, D_MODEL), lambda i: (0, 0)), mod(3), mod(4)],
        out_specs=[rows, rows][:n_out],
        out_shape=[jax.ShapeDtypeStruct((T, D_MODEL), F32), jax.ShapeDtypeStruct((T, D_MODEL), BF16)][:n_out],
        compiler_params=_cparams(("parallel",)),
        name="out_proj",
    )(y, w, *x, mods, g.reshape(1, D_MODEL), mods, mods)


def _swiglu_accumulate(x, wg_ref, wu_ref, wd_ref, o_ref, j):
    g = _dot(x, wg_ref[...].astype(BF16))
    u = _dot(x, wu_ref[...].astype(BF16))
    h = (g * jax.nn.sigmoid(g) * u).astype(BF16)

    @pl.when(j == 0)
    def _():
        o_ref[...] = _dot(h, wd_ref[...].astype(BF16))

    @pl.when(j > 0)
    def _():
        o_ref[...] += _dot(h, wd_ref[...].astype(BF16))


def _ffn_dense_kernel(x_ref, wg_ref, wu_ref, wd_ref, o_ref):
    _swiglu_accumulate(x_ref[...], wg_ref, wu_ref, wd_ref, o_ref, pl.program_id(1))


def _ffn_dense(h, wg, wu, wd):
    tm = TM_FFN
    return pl.pallas_call(
        _ffn_dense_kernel,
        grid=(T // tm, NF),
        in_specs=[pl.BlockSpec((tm, D_MODEL), lambda i, j: (i, 0)),
                  pl.BlockSpec((D_MODEL, TF), lambda i, j: (0, j)),
                  pl.BlockSpec((D_MODEL, TF), lambda i, j: (0, j)),
                  pl.BlockSpec((TF, D_MODEL), lambda i, j: (j, 0))],
        out_specs=pl.BlockSpec((tm, D_MODEL), lambda i, j: (i, 0)),
        out_shape=jax.ShapeDtypeStruct((T, D_MODEL), F32),
        compiler_params=_cparams(("parallel", "arbitrary")),
        name="ffn_dense",
    )(h, wg, wu, wd)


def _norm_router_kernel(x_ref, g_ref, sh_ref, sc_ref, whi_ref, wlo_ref, h_ref, meta_ref, wt_ref, cnt_ref, carry_ref):
    i = pl.program_id(0)

    @pl.when(i == 0)
    def _():
        carry_ref[...] = jnp.zeros_like(carry_ref)

    h = _rms(x_ref[...], g_ref[...]) * (1 + sc_ref[...]) + sh_ref[...]
    h_ref[...] = h
    hi = h.astype(BF16)
    lo = (h - hi.astype(F32)).astype(BF16)
    logits = _dot(hi, whi_ref[...]) + _dot(lo, whi_ref[...]) + _dot(hi, wlo_ref[...])
    lane = lax.broadcasted_iota(jnp.int32, logits.shape, 1)
    lg = jnp.where(lane < N_EXPERTS, logits, -jnp.inf)
    v1 = jnp.max(lg, axis=-1, keepdims=True)
    i1 = jnp.min(jnp.where(lg == v1, lane, LANES), axis=-1, keepdims=True)
    lg2 = jnp.where(lane == i1, -jnp.inf, lg)
    v2 = jnp.max(lg2, axis=-1, keepdims=True)
    i2 = jnp.min(jnp.where(lg2 == v2, lane, LANES), axis=-1, keepdims=True)
    e2 = jnp.exp(v2 - v1)
    tot = 1.0 + e2
    wt_ref[...] = jnp.where(lane == 0, 1.0 / tot, jnp.where(lane == 1, e2 / tot, 0.0))
    oh1 = (lane == i1).astype(F32)
    oh2 = (lane == i2).astype(F32)
    r = lax.broadcasted_iota(jnp.int32, (TR, TR), 0)
    c = lax.broadcasted_iota(jnp.int32, (TR, TR), 1)
    earlier = (r > c).astype(BF16)
    base = carry_ref[0:1, :]
    c1 = jnp.sum(oh1, axis=0, keepdims=True)
    c2 = jnp.sum(oh2, axis=0, keepdims=True)
    r1 = jnp.sum(oh1 * (base + _dot(earlier, oh1.astype(BF16))), axis=-1, keepdims=True)
    r2 = jnp.sum(oh2 * (base + c1 + _dot(earlier, oh2.astype(BF16))), axis=-1, keepdims=True)
    total = jnp.broadcast_to(base + c1 + c2, carry_ref.shape)
    carry_ref[...] = total
    cnt_ref[...] = total
    meta_ref[...] = jnp.where(lane == 0, i1, jnp.where(lane == 1, i2, jnp.where(
        lane == 2, r1.astype(jnp.int32), jnp.where(lane == 3, r2.astype(jnp.int32), 0))))


def _norm_router(x, g, mods, w_hi, w_lo):
    rows = lambda c: pl.BlockSpec((TR, c), lambda i: (i, 0))
    full = lambda shape: pl.BlockSpec(shape, lambda i: (0, 0))
    mod = lambda k: pl.BlockSpec((None, 1, D_MODEL), lambda i: (_cond_row(i, TR), 0, k))
    return pl.pallas_call(
        _norm_router_kernel,
        grid=(T // TR,),
        in_specs=[rows(D_MODEL), full((1, D_MODEL)), mod(3), mod(4), full((D_MODEL, LANES)), full((D_MODEL, LANES))],
        out_specs=[rows(D_MODEL), rows(LANES), rows(LANES), full((8, LANES))],
        out_shape=[jax.ShapeDtypeStruct((T, D_MODEL), F32),
                   jax.ShapeDtypeStruct((T, LANES), jnp.int32),
                   jax.ShapeDtypeStruct((T, LANES), F32),
                   jax.ShapeDtypeStruct((8, LANES), F32)],
        scratch_shapes=[pltpu.VMEM((8, LANES), F32)],
        compiler_params=_cparams(("arbitrary",)),
        name="norm_router",
    )(x, g.reshape(1, D_MODEL), mods, mods, w_hi, w_lo)


def _dispatch(meta, counts):
    experts = jnp.arange(N_EXPERTS, dtype=jnp.int32)
    cnt = counts[0, :N_EXPERTS].astype(jnp.int32)
    padded = ((cnt + TM_FFN - 1) // TM_FFN) * TM_FFN
    g_end = jnp.sum(jnp.where(experts[None, :] <= experts[:, None], padded[None, :], 0), axis=1)
    g_start = g_end - padded
    e = meta[:, :TOP_K]
    start_of = jnp.sum(jnp.where(e[:, :, None] == experts[None, None, :], g_start[None, None, :], 0), axis=-1)
    slot = start_of + meta[:, TOP_K:2 * TOP_K]
    end_tiles = g_end // TM_FFN
    n_used = end_tiles[N_EXPERTS - 1]
    tiles = jnp.arange(MOE_TILES, dtype=jnp.int32)
    owner = lambda t: jnp.minimum(jnp.sum((end_tiles[None, :] <= t[:, None]).astype(jnp.int32), axis=1), N_EXPERTS - 1)
    tile_e = owner(jnp.minimum(tiles, n_used - 1))
    return slot, g_end, padded, tile_e, n_used.reshape(1)


def _row_copy(src, src_row, dst, dst_row, sem):
    return pltpu.make_async_copy(src.at[pl.ds(src_row, 1), :], dst.at[pl.ds(dst_row, 1), :], sem)


def _moe_scatter_kernel(ge_ref, pd_ref, slot_hbm, h_ref, xs_hbm, slot_smem, zero_ref, sem_ids, sem_rows, sem_zero):
    i = pl.program_id(0)

    def clear_copies(first_row):
        base = pl.multiple_of(first_row, ZR)
        return [pltpu.make_async_copy(zero_ref, xs_hbm.at[pl.ds(base + k * ZR, ZR), :], sem_zero)
                for k in range(TM_FFN // ZR)]

    def for_unused_tiles(fn):
        def body(t, carry):
            for cp in clear_copies(t * TM_FFN):
                fn(cp)
            return carry
        lax.fori_loop(ge_ref[N_EXPERTS - 1] // TM_FFN, MOE_TILES, body, 0)

    @pl.when(i == 0)
    def _():
        zero_ref[...] = jnp.zeros_like(zero_ref)
        for e in range(N_EXPERTS):
            @pl.when(pd_ref[e] > 0)
            def _():
                for cp in clear_copies(ge_ref[e] - TM_FFN):
                    cp.start()
        for_unused_tiles(lambda cp: cp.start())
        for e in range(N_EXPERTS):
            @pl.when(pd_ref[e] > 0)
            def _():
                for cp in clear_copies(ge_ref[e] - TM_FFN):
                    cp.wait()
        for_unused_tiles(lambda cp: cp.wait())

    ids_copy = pltpu.make_async_copy(slot_hbm.at[i], slot_smem, sem_ids)
    ids_copy.start()
    ids_copy.wait()

    def start(r, carry):
        _row_copy(h_ref, r, xs_hbm, slot_smem[TOP_K * r], sem_rows).start()
        _row_copy(h_ref, r, xs_hbm, slot_smem[TOP_K * r + 1], sem_rows).start()
        return carry

    lax.fori_loop(0, TR, start, 0)

    def wait(r, carry):
        _row_copy(h_ref, r, xs_hbm, 0, sem_rows).wait()
        _row_copy(h_ref, r, xs_hbm, 0, sem_rows).wait()
        return carry

    lax.fori_loop(0, TR, wait, 0)


def _moe_scatter(h, slot, g_end, padded):
    grid_spec = pltpu.PrefetchScalarGridSpec(
        num_scalar_prefetch=2,
        grid=(T // TR,),
        in_specs=[pl.BlockSpec(memory_space=pl.ANY),
                  pl.BlockSpec((TR, D_MODEL), lambda i, ge, pd: (i, 0))],
        out_specs=pl.BlockSpec(memory_space=pl.ANY),
        scratch_shapes=[pltpu.SMEM((TOP_K * TR,), jnp.int32),
                        pltpu.VMEM((ZR, D_MODEL), F32),
                        pltpu.SemaphoreType.DMA,
                        pltpu.SemaphoreType.DMA,
                        pltpu.SemaphoreType.DMA],
    )
    return pl.pallas_call(
        _moe_scatter_kernel,
        grid_spec=grid_spec,
        out_shape=jax.ShapeDtypeStruct((MOE_TILES * TM_FFN, D_MODEL), F32),
        compiler_params=_cparams(("arbitrary",)),
        name="moe_scatter",
    )(g_end, padded, slot.reshape(T // TR, TOP_K * TR), h)


def _ffn_moe_kernel(te_ref, nu_ref, xs_hbm, wg_ref, wu_ref, wd_ref, o_ref, xg_ref, xb_ref, sem):
    i = pl.program_id(0)
    j = pl.program_id(1)
    n_used = nu_ref[0]
    used = i < n_used

    def tile_copy(t):
        return pltpu.make_async_copy(xs_hbm.at[pl.ds(pl.multiple_of(t * TM_FFN, TM_FFN), TM_FFN), :], xg_ref, sem)

    @pl.when(used & (j == 0))
    def _():
        @pl.when(i == 0)
        def _():
            tile_copy(0).start()

        tile_copy(i).wait()
        xb_ref[...] = xg_ref[...].astype(BF16)

        @pl.when(i + 1 < n_used)
        def _():
            tile_copy(i + 1).start()

    @pl.when(used)
    def _():
        _swiglu_accumulate(xb_ref[...], wg_ref, wu_ref, wd_ref, o_ref, j)

    @pl.when(jnp.logical_not(used) & (j == 0))
    def _():
        o_ref[...] = jnp.zeros_like(o_ref)


def _ffn_moe(xs, tile_e, n_used, wg, wu, wd):
    nf = D_FF // TF_MOE

    def hid(i, j, te, nu):
        return jnp.where(i < nu[0], j, nf - 1)

    grid_spec = pltpu.PrefetchScalarGridSpec(
        num_scalar_prefetch=2,
        grid=(MOE_TILES, nf),
        in_specs=[pl.BlockSpec(memory_space=pl.ANY),
                  pl.BlockSpec((None, None, D_MODEL, TF_MOE), lambda i, j, te, nu: (0, te[i], 0, hid(i, j, te, nu))),
                  pl.BlockSpec((None, None, D_MODEL, TF_MOE), lambda i, j, te, nu: (0, te[i], 0, hid(i, j, te, nu))),
                  pl.BlockSpec((None, None, TF_MOE, D_MODEL), lambda i, j, te, nu: (0, te[i], hid(i, j, te, nu), 0))],
        out_specs=pl.BlockSpec((TM_FFN, D_MODEL), lambda i, j, te, nu: (i, 0)),
        scratch_shapes=[pltpu.VMEM((TM_FFN, D_MODEL), F32),
                        pltpu.VMEM((TM_FFN, D_MODEL), BF16),
                        pltpu.SemaphoreType.DMA],
    )
    return pl.pallas_call(
        _ffn_moe_kernel,
        grid_spec=grid_spec,
        out_shape=jax.ShapeDtypeStruct((MOE_TILES * TM_FFN, D_MODEL), F32),
        compiler_params=_cparams(("arbitrary", "arbitrary")),
        name="ffn_moe",
    )(tile_e, n_used, xs, wg, wu, wd)


def _combine_kernel(slot_hbm, ys_hbm, wt_ref, x_ref, gate_ref, g_ref, yp_ref, ysm_ref,
                    slot_smem, y0_ref, y1_ref, sem_ids, sem_rows, *, n_prompt_tiles):
    i = pl.program_id(0)

    def fetch(t):
        b = t % 2
        ids_copy = pltpu.make_async_copy(slot_hbm.at[t], slot_smem.at[b], sem_ids)
        ids_copy.start()
        ids_copy.wait()

        def start(r, carry):
            _row_copy(ys_hbm, slot_smem[b, TOP_K * r], y0_ref.at[b], r, sem_rows.at[b]).start()
            _row_copy(ys_hbm, slot_smem[b, TOP_K * r + 1], y1_ref.at[b], r, sem_rows.at[b]).start()
            return carry

        lax.fori_loop(0, TC, start, 0)

    @pl.when(i == 0)
    def _():
        fetch(0)

    @pl.when(i + 1 < pl.num_programs(0))
    def _():
        fetch(i + 1)

    b = i % 2

    def wait(r, carry):
        _row_copy(ys_hbm, 0, y0_ref.at[b], r, sem_rows.at[b]).wait()
        _row_copy(ys_hbm, 0, y1_ref.at[b], r, sem_rows.at[b]).wait()
        return carry

    lax.fori_loop(0, TC, wait, 0)
    wt = wt_ref[...]
    y = wt[:, 0:1] * y0_ref[b] + wt[:, 1:2] * y1_ref[b]
    out = _rms(x_ref[...] + gate_ref[...] * y, g_ref[...])

    @pl.when(i < n_prompt_tiles)
    def _():
        yp_ref[...] = out

    @pl.when(i >= n_prompt_tiles)
    def _():
        ysm_ref[...] = out


def _combine(slot, ys, wt, x, mods, k_gate, g):
    n_p = TP // TC
    return pl.pallas_call(
        functools.partial(_combine_kernel, n_prompt_tiles=n_p),
        grid=(T // TC,),
        in_specs=[pl.BlockSpec(memory_space=pl.ANY),
                  pl.BlockSpec(memory_space=pl.ANY),
                  pl.BlockSpec((TC, LANES), lambda i: (i, 0)),
                  pl.BlockSpec((TC, D_MODEL), lambda i: (i, 0)),
                  pl.BlockSpec((None, 1, D_MODEL), lambda i: (_cond_row(i, TC), 0, k_gate)),
                  pl.BlockSpec((1, D_MODEL), lambda i: (0, 0))],
        out_specs=[pl.BlockSpec((TC, D_MODEL), lambda i: (jnp.minimum(i, n_p - 1), 0)),
                   pl.BlockSpec((TC, D_MODEL), lambda i: (jnp.maximum(i - n_p, 0), 0))],
        out_shape=[jax.ShapeDtypeStruct((TP, D_MODEL), F32), jax.ShapeDtypeStruct((TS, D_MODEL), F32)],
        scratch_shapes=[pltpu.SMEM((2, TOP_K * TC), jnp.int32),
                        pltpu.VMEM((2, TC, D_MODEL), F32),
                        pltpu.VMEM((2, TC, D_MODEL), F32),
                        pltpu.SemaphoreType.DMA,
                        pltpu.SemaphoreType.DMA((2,))],
        compiler_params=_cparams(("arbitrary",)),
        name="moe_combine",
    )(slot.reshape(T // TC, TOP_K * TC), ys, wt, x, mods, g.reshape(1, D_MODEL))


def kernel(x_prompt, x_sample, cache_attn_k, cache_attn_v, cache_mla_ckv, cache_mla_krope, c, c_ctx, ln1_g, ln2_g, w_ada, b_ada, w_in, attn_sink, mla_q_norm_g, w_uq, mla_kv_norm_g, w_ukv, pool_w, pool_scale, w_branch_a, w_branch_b, w_branch_c, w_out, ffn_w_gate, ffn_w_up, ffn_w_down, router_w, moe_w_gate, moe_w_up, moe_w_down, final_g):
    x = (x_prompt.reshape(TP, D_MODEL), x_sample.reshape(TS, D_MODEL))
    w_in_t = jnp.swapaxes(w_in, 1, 2)
    cond =jnp.concatenate([c_ctx[None, :], c, jnp.zeros((N_COND - 1 - DEC_BATCH, D_MODEL), F32)], axis=0)
    rows = DEC_SEQ // GRID_W
    cos_a, sin_a = _rope_tables(rows, HD_A)
    cos_b, sin_b = _rope_tables(rows, QK_ROPE)
    st_k, st_v, st_ckv, st_kr = [], [], [], []
    assert DEPTH == 2, "layer 0 is the dense-FFN layer, layer 1 the expert layer that ends the trunk"
    all_mods = [_adaln(cond, w_ada, b_ada, l).reshape(N_COND, 1, 6 * D_MODEL) for l in range(DEPTH)]
    cos_t = jnp.concatenate([jnp.ones((TM, LANES), F32), cos_b], axis=0)
    sin_t = jnp.concatenate([jnp.zeros((TM, LANES), F32), sin_b], axis=0)
    h1 = _norm_mod(x[0], x[1], ln1_g[0], all_mods[0], 0, 1)
    for l in range(DEPTH):
        mods = all_mods[l]
        qa = _matmul_wt(h1, w_in_t, l, 0, A_W, A_W, F32, "proj_q")
        kv = _matmul_wt(h1, w_in_t, l, OFF_K, 2 * KV_W, 2 * KV_W, F32, "proj_kv")
        zb = _matmul_wt(h1, w_in_t, l, OFF_CQ, OFF_POOL - OFF_CQ, OFF_POOL - OFF_CQ, F32, "proj_mla")
        u = _matmul_wt(h1, w_in_t, l, OFF_POOL, C_W, C_W, F32, "proj_pool")
        gates = _matmul_wt(h1, w_in_t, l, OFF_GATE, 3 * D_MODEL, 1024, BF16, "proj_gates")
        sink = attn_sink[l]
        ck = cache_attn_k[:, l].reshape(DEC_BATCH, PAST_LEN, KV_W)
        cv = cache_attn_v[:, l].reshape(DEC_BATCH, PAST_LEN, KV_W)
        oa = (_attn_a_prompt(qa, kv, sink), _attn_a_sample(qa, kv, ck, cv, sink, cos_a, sin_a))
        wq_cat, w_aug = _mla_weights(w_uq[l], w_ukv[l])
        q, ckv, kr, xa = _mla_prep(zb, mla_q_norm_g[l], mla_kv_norm_g[l], wq_cat, cos_t, sin_t)
        cache_aug = jnp.concatenate([cache_mla_ckv[:, l], cache_mla_krope[:, l]], axis=-1).astype(BF16)
        xa_s = jnp.concatenate([xa[TP:].reshape(DEC_BATCH, DEC_SEQ, MLA_AUG), cache_aug], axis=1)
        kv_p = _matmul(xa[:TP], w_aug, BF16, TM, B_W, "mla_kv_prompt")
        kv_s = _matmul(xa_s.reshape(DEC_BATCH * MLA_KEYS, MLA_AUG), w_aug, BF16, TM, B_W, "mla_kv_sample")
        ob = (_mla_attend(q, kv_p, SEQ, 0, BATCH, 1, SEQ, "mla_prompt"),
              _mla_attend(q, kv_s, TQ_MLA, TP // TQ_MLA, DEC_BATCH, DEC_SEQ // TQ_MLA, MLA_KEYS, "mla_sample"))
        pw = pool_w[l].astype(BF16)
        oc = (_pool(u, pw, pool_scale[l], SEQ, BATCH, 0, "pool_prompt"),
              _pool(u, pw, pool_scale[l], DEC_SEQ, DEC_BATCH, TP // DEC_SEQ, "pool_sample"))
        y = _merge((oa[0], ob[0], oc[0]), (oa[1], ob[1], oc[1]), gates, w_branch_a[l].astype(BF16),
                   w_branch_b[l].astype(BF16), w_branch_c[l].astype(BF16))
        st_k.append(kv[:TP, :KV_W].reshape(BATCH, SEQ, N_KV_A, HD_A))
        st_v.append(kv[:TP, KV_W:].reshape(BATCH, SEQ, N_KV_A, HD_A))
        st_ckv.append(ckv[:TP].reshape(BATCH, SEQ, KV_LORA))
        st_kr.append(kr[:TP].reshape(BATCH, SEQ, QK_ROPE))
        if l == 0:
            x, h2 = _out_proj(y, w_out[l].astype(BF16), x, mods, ln2_g[l], True)
            f = _ffn_dense(h2, ffn_w_gate[0].astype(BF16), ffn_w_up[0].astype(BF16), ffn_w_down[0].astype(BF16))
            x, h1 = _resid_norm(x, f, mods, 5, ln1_g[1], all_mods[1], 0, 1)
        else:
            (x,) = _out_proj(y, w_out[l].astype(BF16), x, mods, ln2_g[l], False)
            rw = jnp.pad(router_w[0], ((0, 0), (0, LANES - N_EXPERTS)))
            rw_hi = rw.astype(BF16)
            rw_lo = (rw - rw_hi.astype(F32)).astype(BF16)
            h2, meta, wt, counts = _norm_router(x, ln2_g[l], mods, rw_hi, rw_lo)
            slot, g_end, padded, tile_e, n_used = _dispatch(meta, counts)
            xs = _moe_scatter(h2, slot, g_end, padded)
            ys = _ffn_moe(xs, tile_e, n_used, moe_w_gate, moe_w_up, moe_w_down)
            y_p, y_s = _combine(slot, ys, wt, x, mods, 5, final_g)
    return (y_p.reshape(BATCH, SEQ, D_MODEL), y_s.reshape(DEC_BATCH, DEC_SEQ, D_MODEL),
            jnp.stack(st_k, axis=1), jnp.stack(st_v, axis=1), jnp.stack(st_ckv, axis=1), jnp.stack(st_kr, axis=1))
```

```python
import functools
import math

import jax
import jax.numpy as jnp
from jax import lax
from jax.experimental import pallas as pl
from jax.experimental.pallas import tpu as pltpu

BF16 = jnp.bfloat16
F32 = jnp.float32

D_MODEL = 2048
BATCH = 16
SEQ = 256
DEPTH = 2
DEC_BATCH = 8
DEC_SEQ = 2048
PAST_LEN = 512
GRID_W = 64
BLK = 128
EPS = 1e-6
ROPE_BASE = 10000.0
NEG_INF = -1e30
N_HEADS_A = 16
N_KV_A = 2
HD_A = 64
WINDOW = 128
A_W = N_HEADS_A * HD_A
KV_W = N_KV_A * HD_A
N_HEADS_B = 8
Q_LORA = 512
KV_LORA = 256
QK_NOPE = 128
QK_ROPE = 64
V_HD = 128
B_W = N_HEADS_B * V_HD
POOL_WINDOWS = (2, 4, 8, 16)
POOL_GROUPS = 4
POOL_GW = 256
C_W = POOL_GROUPS * POOL_GW
OFF_K = A_W
OFF_V = OFF_K + KV_W
OFF_CQ = OFF_V + KV_W
OFF_CKV = OFF_CQ + Q_LORA
OFF_KR = OFF_CKV + KV_LORA
OFF_POOL = OFF_KR + QK_ROPE
OFF_GATE = OFF_POOL + C_W
IN_COLS = OFF_GATE + 3 * D_MODEL
D_FF = 5632
N_EXPERTS = 8
TOP_K = 2

TP = BATCH * SEQ
TS = DEC_BATCH * DEC_SEQ
T = TP + TS
N_COND = 16
MLA_KEYS = DEC_SEQ + PAST_LEN
LANES = 128
POOL_HALO = 8
LOG2E = math.log2(math.e)
MLA_HEAD_K = 256
MLA_AUG = KV_LORA + QK_ROPE

VMEM_LIMIT = 56 * 1024 * 1024

TM = 1024
TM_FFN = 1024
TF = 512
NF = D_FF // TF
TF_MOE = 256
MOE_TILES = (TOP_K * T) // TM_FFN + N_EXPERTS
TR = 512
ZR = 256
TC = 256
TQ_MLA = 512
TQ_A = 256


def _cparams(sem):
    return pltpu.CompilerParams(dimension_semantics=sem, vmem_limit_bytes=VMEM_LIMIT)


def _cond_row(i, tm):
    n_p = TP // tm
    per_b = DEC_SEQ // tm
    return jnp.where(i < n_p, 0, (i - n_p) // per_b + 1)


def _dot(a, b):
    return jnp.dot(a, b, preferred_element_type=F32)


def _dot_nt(a, b):
    return lax.dot_general(a, b, (((1,), (1,)), ((), ())), preferred_element_type=F32)


def _mm_kernel(x_ref, w_ref, o_ref):
    o_ref[...] = _dot(x_ref[...].astype(BF16), w_ref[...].astype(BF16)).astype(o_ref.dtype)


def _matmul(x, w, out_dtype, tm, tn, name):
    m, k = x.shape
    n = w.shape[1]
    return pl.pallas_call(
        _mm_kernel,
        grid=(m // tm, n // tn),
        in_specs=[pl.BlockSpec((tm, k), lambda i, j: (i, 0)),
                  pl.BlockSpec((k, tn), lambda i, j: (0, j))],
        out_specs=pl.BlockSpec((tm, tn), lambda i, j: (i, j)),
        out_shape=jax.ShapeDtypeStruct((m, n), out_dtype),
        compiler_params=_cparams(("parallel", "arbitrary")),
        name=name,
    )(x, w)


def _mm_wt_kernel(x_ref, wt_hbm, o_ref, stage_ref, wb_ref, sem, *, layer, row0, tn):
    j = pl.program_id(0)

    def tile_copy(t):
        first = pl.multiple_of(row0 + t * tn, 8)
        return pltpu.make_async_copy(wt_hbm.at[layer, pl.ds(first, tn), :], stage_ref, sem)

    @pl.when(pl.program_id(1) == 0)
    def _():
        @pl.when(j == 0)
        def _():
            tile_copy(0).start()

        tile_copy(j).wait()
        wb_ref[...] = stage_ref[...].astype(BF16)

        @pl.when(j + 1 < pl.num_programs(0))
        def _():
            tile_copy(j + 1).start()

    o_ref[...] = _dot_nt(x_ref[...], wb_ref[...]).astype(o_ref.dtype)


def _matmul_wt(x, wt, layer, row0, n, tn, out_dtype, name):
    m, k = x.shape
    tm = TM
    assert n % tn == 0 and row0 % 8 == 0 and tn % 16 == 0
    return pl.pallas_call(
        functools.partial(_mm_wt_kernel, layer=layer, row0=row0, tn=tn),
        grid=(n // tn, m // tm),
        in_specs=[pl.BlockSpec((tm, k), lambda j, i: (i, 0)), pl.BlockSpec(memory_space=pl.ANY)],
        out_specs=pl.BlockSpec((tm, tn), lambda j, i: (i, j)),
        out_shape=jax.ShapeDtypeStruct((m, n), out_dtype),
        scratch_shapes=[pltpu.VMEM((tn, k), F32), pltpu.VMEM((tn, k), BF16), pltpu.SemaphoreType.DMA],
        compiler_params=_cparams(("arbitrary", "arbitrary")),
        name=name,
    )(x, wt)


def _ada_kernel(c_ref, w_ref, b_ref, o_ref):
    c = c_ref[...]
    a = (c * jax.nn.sigmoid(c)).astype(BF16)
    o_ref[...] = _dot(a, w_ref[...].astype(BF16)) + b_ref[...]


def _adaln(cond, w, b, layer):
    n = w.shape[2]
    tn = 1024
    return pl.pallas_call(
        _ada_kernel,
        grid=(n // tn,),
        in_specs=[pl.BlockSpec((N_COND, D_MODEL), lambda j: (0, 0)),
                  pl.BlockSpec((None, D_MODEL, tn), lambda j: (layer, 0, j)),
                  pl.BlockSpec((None, 1, tn), lambda j: (layer, 0, j))],
        out_specs=pl.BlockSpec((N_COND, tn), lambda j: (0, j)),
        out_shape=jax.ShapeDtypeStruct((N_COND, n), F32),
        compiler_params=_cparams(("arbitrary",)),
        name="adaln",
    )(cond, w, b.reshape(DEPTH, 1, n))


def _split_row_specs(tm, width):
    n_p = TP // tm
    return [pl.BlockSpec((tm, width), lambda i, *_: (jnp.minimum(i, n_p - 1), 0)),
            pl.BlockSpec((tm, width), lambda i, *_: (jnp.maximum(i - n_p, 0), 0))], n_p


def _pick_rows(p_ref, s_ref, n_prompt_tiles):
    return jnp.where(pl.program_id(0) < n_prompt_tiles, p_ref[...], s_ref[...])


def _norm_mod_kernel(xp_ref, xs_ref, g_ref, sh_ref, sc_ref, o_ref, *, n_prompt_tiles):
    x = _pick_rows(xp_ref, xs_ref, n_prompt_tiles)
    o_ref[...] = (_rms(x, g_ref[...]) * (1 + sc_ref[...]) + sh_ref[...]).astype(o_ref.dtype)


def _norm_mod(xp, xs, g, mods, k_shift, k_scale):
    tm = 512
    x_specs, n_p = _split_row_specs(tm, D_MODEL)
    return pl.pallas_call(
        functools.partial(_norm_mod_kernel, n_prompt_tiles=n_p),
        grid=(T // tm,),
        in_specs=x_specs + [pl.BlockSpec((1, D_MODEL), lambda i: (0, 0)),
                            pl.BlockSpec((None, 1, D_MODEL), lambda i: (_cond_row(i, tm), 0, k_shift)),
                            pl.BlockSpec((None, 1, D_MODEL), lambda i: (_cond_row(i, tm), 0, k_scale))],
        out_specs=pl.BlockSpec((tm, D_MODEL), lambda i: (i, 0)),
        out_shape=jax.ShapeDtypeStruct((T, D_MODEL), BF16),
        compiler_params=_cparams(("parallel",)),
        name="norm_mod1",
    )(xp, xs, g.reshape(1, D_MODEL), mods, mods)


def _rms(x, g):
    return x * lax.rsqrt(jnp.mean(x * x, axis=-1, keepdims=True) + EPS) * g


def _resid_norm_kernel(x_ref, f_ref, gate_ref, g_ref, sh_ref, sc_ref, x2_ref, h_ref):
    x2 = x_ref[...] + gate_ref[...] * f_ref[...]
    x2_ref[...] = x2
    h_ref[...] = (_rms(x2, g_ref[...]) * (1 + sc_ref[...]) + sh_ref[...]).astype(h_ref.dtype)


def _resid_norm(x, f, mods, k_gate, g, mods_next, k_shift, k_scale):
    tm = 512
    rows = pl.BlockSpec((tm, D_MODEL), lambda i: (i, 0))
    mod = lambda k: pl.BlockSpec((None, 1, D_MODEL), lambda i: (_cond_row(i, tm), 0, k))
    return pl.pallas_call(
        _resid_norm_kernel,
        grid=(T // tm,),
        in_specs=[rows, rows, mod(k_gate), pl.BlockSpec((1, D_MODEL), lambda i: (0, 0)), mod(k_shift), mod(k_scale)],
        out_specs=[rows, rows],
        out_shape=[jax.ShapeDtypeStruct((T, D_MODEL), F32), jax.ShapeDtypeStruct((T, D_MODEL), BF16)],
        compiler_params=_cparams(("parallel",)),
        name="resid_norm",
    )(x, f, mods, g.reshape(1, D_MODEL), mods_next, mods_next)


def _rope_tables(rows, dim):
    quarter = dim // 4
    inv = ROPE_BASE ** (-jnp.arange(quarter, dtype=F32) / quarter)
    r = jnp.repeat(jnp.arange(rows, dtype=F32), GRID_W)
    col = jnp.tile(jnp.arange(GRID_W, dtype=F32), rows)
    ar, ac = r[:, None] * inv, col[:, None] * inv
    cos = jnp.concatenate([jnp.cos(ar), jnp.cos(ar), jnp.cos(ac), jnp.cos(ac)], axis=-1)
    sin = jnp.concatenate([-jnp.sin(ar), jnp.sin(ar), -jnp.sin(ac), jnp.sin(ac)], axis=-1)
    reps = LANES // dim
    return jnp.tile(cos, (1, reps)), jnp.tile(sin, (1, reps))


def _rope_lanes(x, cos, sin, quarter):
    lane = lax.broadcasted_iota(jnp.int32, (x.shape[0], LANES), 1)
    first = (lane % (2 * quarter)) < quarter
    outs = []
    for c in range(x.shape[1] // LANES):
        xc = x[:, c * LANES:(c + 1) * LANES]
        partner = jnp.where(first, pltpu.roll(xc, LANES - quarter, 1), pltpu.roll(xc, quarter, 1))
        outs.append(xc * cos + partner * sin)
    return outs[0] if len(outs) == 1 else jnp.concatenate(outs, axis=1)


def _attn_a_kernel(*refs, local, tq, n_blocks):
    assert N_KV_A * HD_A == LANES
    if local:
        (sink_ref, q_ref, kc_ref, vc_ref, kvm_ref, kv0_ref, kvp_ref,
         cq_ref, sq_ref, ckm_ref, skm_ref, ck0_ref, sk0_ref, ckp_ref, skp_ref, o_ref) = refs
    else:
        sink_ref, q_ref, kc_ref, vc_ref, o_ref = refs
    quarter = HD_A // 4
    q = q_ref[...]
    k_all = kc_ref[...]
    v_all = vc_ref[...]
    n_loc = 0
    if local:
        j = pl.program_id(1)
        n_loc = tq + 2 * BLK
        q = _rope_lanes(q, cq_ref[...], sq_ref[...], quarter)
        k_all = jnp.concatenate([
            _rope_lanes(kvm_ref[:, :KV_W], ckm_ref[...], skm_ref[...], quarter),
            _rope_lanes(kv0_ref[:, :KV_W], ck0_ref[...], sk0_ref[...], quarter),
            _rope_lanes(kvp_ref[:, :KV_W], ckp_ref[...], skp_ref[...], quarter), k_all], axis=0)
        v_all = jnp.concatenate([kvm_ref[:, KV_W:], kv0_ref[:, KV_W:], kvp_ref[:, KV_W:], v_all], axis=0)
        qi = lax.broadcasted_iota(jnp.int32, (tq, n_loc), 0)
        ki = lax.broadcasted_iota(jnp.int32, (tq, n_loc), 1)
        k_lo = jnp.where(j == 0, BLK, 0)
        k_hi = jnp.where(j == n_blocks - 1, n_loc - BLK, n_loc)
        valid = (ki >= qi) & (ki <= qi + 2 * WINDOW) & (ki >= k_lo) & (ki < k_hi)
        bias = jnp.where(valid, 0.0, NEG_INF)
    q = (q * (HD_A ** -0.5 * LOG2E)).astype(BF16)
    nk = k_all.shape[0]
    low_k = lax.broadcasted_iota(jnp.int32, (nk, LANES), 1) < HD_A
    low_q = lax.broadcasted_iota(jnp.int32, (tq, LANES), 1) < HD_A
    rep = N_HEADS_A // N_KV_A

    def blockdiag(x, g):
        swapped = pltpu.roll(x, HD_A, 1)
        lo_src, hi_src = (x, swapped) if g == 0 else (swapped, x)
        return jnp.concatenate([jnp.where(low_k, lo_src, 0.0), jnp.where(low_k, 0.0, hi_src)], axis=0).astype(BF16)

    for g in range(N_KV_A):
        k2 = blockdiag(k_all, g)
        v2 = blockdiag(v_all, g)
        for p in range(rep // 2):
            h0 = g * rep + 2 * p
            s = _dot_nt(q[:, h0 * HD_A:h0 * HD_A + LANES], k2)
            parts, inv = [], []
            for t in range(2):
                sh = s[:, t * nk:(t + 1) * nk]
                sk = sink_ref[h0 + t] * LOG2E
                if local:
                    s_loc = sh[:, :n_loc] + bias
                    s_ctx = sh[:, n_loc:]
                    m = jnp.maximum(jnp.maximum(jnp.max(s_loc, axis=-1, keepdims=True),
                                                jnp.max(s_ctx, axis=-1, keepdims=True)), sk)
                    p_loc = jnp.exp2(s_loc - m)
                    p_ctx = jnp.exp2(s_ctx - m)
                    denom = (jnp.sum(p_loc, axis=-1, keepdims=True) + jnp.sum(p_ctx, axis=-1, keepdims=True)
                             + jnp.exp2(sk - m))
                    parts += [p_loc, p_ctx]
                else:
                    m = jnp.maximum(jnp.max(sh, axis=-1, keepdims=True), sk)
                    p_all = jnp.exp2(sh - m)
                    denom = jnp.sum(p_all, axis=-1, keepdims=True) + jnp.exp2(sk - m)
                    parts.append(p_all)
                inv.append(1.0 / denom)
            o2 = _dot(jnp.concatenate(parts, axis=1).astype(BF16), v2) * jnp.where(low_q, inv[0], inv[1])
            o_ref[:, h0 * HD_A:h0 * HD_A + LANES] = o2.astype(o_ref.dtype)


def _attn_a_prompt(qa, kv, sink):
    kern = functools.partial(_attn_a_kernel, local=False, tq=SEQ, n_blocks=1)
    return pl.pallas_call(
        kern,
        grid=(BATCH,),
        in_specs=[pl.BlockSpec(memory_space=pltpu.SMEM),
                  pl.BlockSpec((SEQ, A_W), lambda b: (b, 0)),
                  pl.BlockSpec((SEQ, KV_W), lambda b: (b, 0)),
                  pl.BlockSpec((SEQ, KV_W), lambda b: (b, 1))],
        out_specs=pl.BlockSpec((SEQ, A_W), lambda b: (b, 0)),
        out_shape=jax.ShapeDtypeStruct((TP, A_W), BF16),
        compiler_params=_cparams(("parallel",)),
        name="attn_a_prompt",
    )(sink, qa, kv, kv)


def _attn_a_sample(qa, kv, ck, cv, sink, cos, sin):
    tq = TQ_A
    qb = tq // BLK
    nb = DEC_SEQ // BLK
    nq = DEC_SEQ // tq
    base_blk = TP // BLK
    base_q = TP // tq

    def prev_blk(j):
        return jnp.maximum(qb * j - 1, 0)

    def next_blk(j):
        return jnp.minimum(qb * j + qb, nb - 1)

    edge_kv = lambda f: pl.BlockSpec((BLK, 2 * KV_W), lambda b, j: (base_blk + b * nb + f(j), 0))
    edge_tab = lambda f: pl.BlockSpec((BLK, LANES), lambda b, j: (f(j), 0))
    own_tab = pl.BlockSpec((tq, LANES), lambda b, j: (j, 0))
    kern = functools.partial(_attn_a_kernel, local=True, tq=tq, n_blocks=nq)
    return pl.pallas_call(
        kern,
        grid=(DEC_BATCH, nq),
        in_specs=[pl.BlockSpec(memory_space=pltpu.SMEM),
                  pl.BlockSpec((tq, A_W), lambda b, j: (base_q + b * nq + j, 0)),
                  pl.BlockSpec((None, PAST_LEN, KV_W), lambda b, j: (b, 0, 0)),
                  pl.BlockSpec((None, PAST_LEN, KV_W), lambda b, j: (b, 0, 0)),
                  edge_kv(prev_blk),
                  pl.BlockSpec((tq, 2 * KV_W), lambda b, j: (base_q + b * nq + j, 0)),
                  edge_kv(next_blk),
                  own_tab, own_tab, edge_tab(prev_blk), edge_tab(prev_blk),
                  own_tab, own_tab, edge_tab(next_blk), edge_tab(next_blk)],
        out_specs=pl.BlockSpec((tq, A_W), lambda b, j: (b * nq + j, 0)),
        out_shape=jax.ShapeDtypeStruct((TS, A_W), BF16),
        compiler_params=_cparams(("parallel", "arbitrary")),
        name="attn_a_sample",
    )(sink, qa, ck, cv, kv, kv, kv, cos, sin, cos, sin, cos, sin, cos, sin)


def _mla_prep_kernel(z_ref, gq_ref, gkv_ref, wq_ref, cos_ref, sin_ref, q_ref, ckv_ref, kr_ref, xa_ref):
    quarter = QK_ROPE // 4
    c = (QK_NOPE + QK_ROPE) ** -0.5 * LOG2E
    cos = cos_ref[...]
    sin = sin_ref[...]
    cq = z_ref[:, :Q_LORA]
    q = _dot(_rms(cq, gq_ref[...]).astype(BF16), wq_ref[...])
    for h in range(N_HEADS_B):
        lo = h * MLA_HEAD_K
        q_ref[:, lo:lo + QK_NOPE] = (q[:, lo:lo + QK_NOPE] * c).astype(q_ref.dtype)
        rot = _rope_lanes(q[:, lo + QK_NOPE:lo + MLA_HEAD_K], cos, sin, quarter)
        q_ref[:, lo + QK_NOPE:lo + MLA_HEAD_K] = (rot * c).astype(q_ref.dtype)
    ckv = _rms(z_ref[:, Q_LORA:Q_LORA + KV_LORA], gkv_ref[...])
    ckv_ref[...] = ckv
    kr = z_ref[:, Q_LORA + KV_LORA:]
    kr_ref[...] = kr
    xa_ref[:, :KV_LORA] = ckv.astype(xa_ref.dtype)
    kr_rot = _rope_lanes(jnp.concatenate([kr, kr], axis=1), cos, sin, quarter)
    xa_ref[:, KV_LORA:] = kr_rot[:, :QK_ROPE].astype(xa_ref.dtype)


def _mla_prep(z, gq, gkv, wq_cat, cos, sin):
    tm = TM
    zc = Q_LORA + KV_LORA + QK_ROPE
    n_p = TP // tm
    per_b = DEC_SEQ // tm
    full = lambda shape: pl.BlockSpec(shape, lambda i: (0, 0))
    rows = lambda c: pl.BlockSpec((tm, c), lambda i: (i, 0))
    tab = pl.BlockSpec((tm, LANES), lambda i: (jnp.where(i < n_p, 0, 1 + (i - n_p) % per_b), 0))
    return pl.pallas_call(
        _mla_prep_kernel,
        grid=(T // tm,),
        in_specs=[rows(zc), full((1, Q_LORA)), full((1, KV_LORA)), full((Q_LORA, N_HEADS_B * MLA_HEAD_K)), tab, tab],
        out_specs=[rows(N_HEADS_B * MLA_HEAD_K), rows(KV_LORA), rows(QK_ROPE), rows(MLA_AUG)],
        out_shape=[jax.ShapeDtypeStruct((T, N_HEADS_B * MLA_HEAD_K), BF16),
                   jax.ShapeDtypeStruct((T, KV_LORA), F32),
                   jax.ShapeDtypeStruct((T, QK_ROPE), F32),
                   jax.ShapeDtypeStruct((T, MLA_AUG), BF16)],
        compiler_params=_cparams(("parallel",)),
        name="mla_prep",
    )(z, gq.reshape(1, Q_LORA), gkv.reshape(1, KV_LORA), wq_cat, cos, sin)


def _mla_weights(w_uq, w_ukv):
    wq = w_uq.reshape(Q_LORA, N_HEADS_B, QK_NOPE + QK_ROPE)
    wq = jnp.pad(wq, ((0, 0), (0, 0), (0, MLA_HEAD_K - QK_NOPE - QK_ROPE))).reshape(Q_LORA, N_HEADS_B * MLA_HEAD_K)
    wkv = w_ukv.reshape(KV_LORA, N_HEADS_B, QK_NOPE + V_HD)
    wk = jnp.pad(wkv[:, :, :QK_NOPE], ((0, QK_ROPE), (0, 0), (0, MLA_HEAD_K - QK_NOPE)))
    eye = jnp.pad(jnp.eye(QK_ROPE, dtype=F32), ((KV_LORA, 0), (QK_NOPE, MLA_HEAD_K - QK_NOPE - QK_ROPE)))
    wk = (wk + eye[:, None, :]).reshape(MLA_AUG, N_HEADS_B * MLA_HEAD_K)
    wv = jnp.pad(wkv[:, :, QK_NOPE:], ((0, QK_ROPE), (0, 0), (0, 0))).reshape(MLA_AUG, B_W)
    return wq.astype(BF16), jnp.concatenate([wk, wv], axis=1).astype(BF16)


def _mla_kernel(q_ref, k_ref, v_ref, o_ref):
    for h in range(N_HEADS_B):
        ks = slice(h * MLA_HEAD_K, (h + 1) * MLA_HEAD_K)
        vs = slice(h * V_HD, (h + 1) * V_HD)
        s = _dot_nt(q_ref[:, ks], k_ref[:, ks])
        p = jnp.exp2(s - jnp.max(s, axis=-1, keepdims=True))
        denom = jnp.sum(p, axis=-1, keepdims=True)
        o_ref[:, vs] = (_dot(p.astype(BF16), v_ref[:, vs]) / denom).astype(o_ref.dtype)


def _mla_attend(q, kv, tq, q_base, n_seq, nq, nk, name):
    kw = N_HEADS_B * MLA_HEAD_K
    return pl.pallas_call(
        _mla_kernel,
        grid=(n_seq, nq),
        in_specs=[pl.BlockSpec((tq, kw), lambda b, i: (q_base + b * nq + i, 0)),
                  pl.BlockSpec((nk, kw), lambda b, i: (b, 0), pipeline_mode=pl.Buffered(1)),
                  pl.BlockSpec((nk, B_W), lambda b, i: (b, kw // B_W), pipeline_mode=pl.Buffered(1))],
        out_specs=pl.BlockSpec((tq, B_W), lambda b, i: (b * nq + i, 0)),
        out_shape=jax.ShapeDtypeStruct((n_seq * nq * tq, B_W), BF16),
        compiler_params=_cparams(("parallel", "arbitrary")),
        name=name,
    )(q, kv, kv)


def _pool_kernel(u_ref, w_ref, s_ref, o_ref, pad_ref, *, n):
    chunk = min(n, 256)
    zeros = jnp.zeros((POOL_HALO, C_W), F32)
    pad_ref[0:POOL_HALO, :] = zeros
    pad_ref[POOL_HALO + n:POOL_HALO + n + POOL_HALO, :] = zeros
    pad_ref[POOL_HALO:POOL_HALO + n, :] = u_ref[...]
    for c in range(n // chunk):
        r0 = c * chunk
        t = lax.broadcasted_iota(jnp.int32, (chunk, 1), 0) + r0
        for g, win in enumerate(POOL_WINDOWS):
            left = win // 2
            right = win - left - 1
            cols = slice(g * POOL_GW, (g + 1) * POOL_GW)
            acc = pad_ref[POOL_HALO + r0 - left:POOL_HALO + r0 - left + chunk, cols]
            for k in range(-left + 1, right + 1):
                acc = acc + pad_ref[POOL_HALO + r0 + k:POOL_HALO + r0 + k + chunk, cols]
            cnt = (jnp.minimum(t + right, n - 1) + 1 - jnp.maximum(t - left, 0)).astype(F32)
            d = acc / cnt - pad_ref[POOL_HALO + r0:POOL_HALO + r0 + chunk, cols]
            y = _dot(d.astype(BF16), w_ref[g]) * s_ref[:, cols]
            o_ref[r0:r0 + chunk, cols] = y.astype(o_ref.dtype)


def _pool(u, w, scale, n, n_seq, row_base, name):
    kern = functools.partial(_pool_kernel, n=n)
    return pl.pallas_call(
        kern,
        grid=(n_seq,),
        in_specs=[pl.BlockSpec((n, C_W), lambda b: (row_base + b, 0)),
                  pl.BlockSpec((POOL_GROUPS, POOL_GW, POOL_GW), lambda b: (0, 0, 0)),
                  pl.BlockSpec((1, C_W), lambda b: (0, 0))],
        out_specs=pl.BlockSpec((n, C_W), lambda b: (b, 0)),
        out_shape=jax.ShapeDtypeStruct((n_seq * n, C_W), BF16),
        scratch_shapes=[pltpu.VMEM((n + 2 * POOL_HALO, C_W), F32)],
        compiler_params=_cparams(("parallel",)),
        name=name,
    )(u, w, scale.reshape(1, C_W))


def _merge_kernel(oap_ref, obp_ref, ocp_ref, oas_ref, obs_ref, ocs_ref, ga_ref, gb_ref, gc_ref,
                  wa_ref, wb_ref, wc_ref, o_ref, *, n_prompt_tiles):
    is_p = pl.program_id(0) < n_prompt_tiles
    pick = lambda p_ref, s_ref: jnp.where(is_p, p_ref[...], s_ref[...])
    y = jax.nn.sigmoid(ga_ref[...].astype(F32)) * _dot(pick(oap_ref, oas_ref), wa_ref[...])
    y = y + jax.nn.sigmoid(gb_ref[...].astype(F32)) * _dot(pick(obp_ref, obs_ref), wb_ref[...])
    y = y + jax.nn.sigmoid(gc_ref[...].astype(F32)) * _dot(pick(ocp_ref, ocs_ref), wc_ref[...])
    o_ref[...] = y.astype(o_ref.dtype)


def _merge(branches_p, branches_s, gates, wa, wb, wc):
    tm, tn = TM, 1024
    nn = D_MODEL // tn
    n_p = TP // tm
    rows_p = pl.BlockSpec((tm, A_W), lambda i, j: (jnp.minimum(i, n_p - 1), 0), pipeline_mode=pl.Buffered(1))
    rows_s = pl.BlockSpec((tm, A_W), lambda i, j: (jnp.maximum(i - n_p, 0), 0))
    gate = lambda k: pl.BlockSpec((tm, tn), lambda i, j: (i, k * nn + j))
    wspec = pl.BlockSpec((A_W, tn), lambda i, j: (0, j))
    return pl.pallas_call(
        functools.partial(_merge_kernel, n_prompt_tiles=n_p),
        grid=(T // tm, nn),
        in_specs=[rows_p] * 3 + [rows_s] * 3 + [gate(0), gate(1), gate(2), wspec, wspec, wspec],
        out_specs=pl.BlockSpec((tm, tn), lambda i, j: (i, j)),
        out_shape=jax.ShapeDtypeStruct((T, D_MODEL), BF16),
        compiler_params=_cparams(("parallel", "arbitrary")),
        name="merge",
    )(*branches_p, *branches_s, gates, gates, gates, wa, wb, wc)


def _out_proj_kernel(*refs, n_x, n_prompt_tiles):
    y_ref, w_ref = refs[:2]
    x_refs = refs[2:2 + n_x]
    gate_ref, g_ref, sh_ref, sc_ref, x1_ref = refs[2 + n_x:7 + n_x]
    h_refs = refs[7 + n_x:]
    x = x_refs[0][...] if n_x == 1 else _pick_rows(x_refs[0], x_refs[1], n_prompt_tiles)
    x1 = x + gate_ref[...] * _dot(y_ref[...], w_ref[...])
    x1_ref[...] = x1
    if h_refs:
        h_refs[0][...] = (_rms(x1, g_ref[...]) * (1 + sc_ref[...]) + sh_ref[...]).astype(h_refs[0].dtype)


def _out_proj(y, w, x, mods, g, emit_h):
    tm = 512
    rows = pl.BlockSpec((tm, D_MODEL), lambda i: (i, 0))
    mod = lambda k: pl.BlockSpec((None, 1, D_MODEL), lambda i: (_cond_row(i, tm), 0, k))
    if isinstance(x, tuple):
        x_specs, n_p = _split_row_specs(tm, D_MODEL)
    else:
        x, x_specs, n_p = (x,), [rows], 0
    n_out = 2 if emit_h else 1
    return pl.pallas_call(
        functools.partial(_out_proj_kernel, n_x=len(x), n_prompt_tiles=n_p),
        grid=(T // tm,),
        in_specs=[rows, pl.BlockSpec((D_MODEL, D_MODEL), lambda i: (0, 0))] + x_specs + [
            mod(2), pl.BlockSpec((1, D_MODEL), lambda i: (0, 0)), mod(3), mod(4)],
        out_specs=[rows, rows][:n_out],
        out_shape=[jax.ShapeDtypeStruct((T, D_MODEL), F32), jax.ShapeDtypeStruct((T, D_MODEL), BF16)][:n_out],
        compiler_params=_cparams(("parallel",)),
        name="out_proj",
    )(y, w, *x, mods, g.reshape(1, D_MODEL), mods, mods)


def _swiglu_accumulate(x, wg_ref, wu_ref, wd_ref, o_ref, j):
    g = _dot(x, wg_ref[...].astype(BF16))
    u = _dot(x, wu_ref[...].astype(BF16))
    h = (g * jax.nn.sigmoid(g) * u).astype(BF16)

    @pl.when(j == 0)
    def _():
        o_ref[...] = _dot(h, wd_ref[...].astype(BF16))

    @pl.when(j > 0)
    def _():
        o_ref[...] += _dot(h, wd_ref[...].astype(BF16))


def _ffn_dense_kernel(x_ref, wg_ref, wu_ref, wd_ref, o_ref):
    _swiglu_accumulate(x_ref[...], wg_ref, wu_ref, wd_ref, o_ref, pl.program_id(1))


def _ffn_dense(h, wg, wu, wd):
    tm = TM_FFN
    return pl.pallas_call(
        _ffn_dense_kernel,
        grid=(T // tm, NF),
        in_specs=[pl.BlockSpec((tm, D_MODEL), lambda i, j: (i, 0)),
                  pl.BlockSpec((D_MODEL, TF), lambda i, j: (0, j)),
                  pl.BlockSpec((D_MODEL, TF), lambda i, j: (0, j)),
                  pl.BlockSpec((TF, D_MODEL), lambda i, j: (j, 0))],
        out_specs=pl.BlockSpec((tm, D_MODEL), lambda i, j: (i, 0)),
        out_shape=jax.ShapeDtypeStruct((T, D_MODEL), F32),
        compiler_params=_cparams(("parallel", "arbitrary")),
        name="ffn_dense",
    )(h, wg, wu, wd)


def _norm_router_kernel(x_ref, g_ref, sh_ref, sc_ref, whi_ref, wlo_ref, h_ref, meta_ref, wt_ref, cnt_ref, carry_ref):
    i = pl.program_id(0)

    @pl.when(i == 0)
    def _():
        carry_ref[...] = jnp.zeros_like(carry_ref)

    h = _rms(x_ref[...], g_ref[...]) * (1 + sc_ref[...]) + sh_ref[...]
    h_ref[...] = h
    hi = h.astype(BF16)
    lo = (h - hi.astype(F32)).astype(BF16)
    logits = _dot(hi, whi_ref[...]) + _dot(lo, whi_ref[...]) + _dot(hi, wlo_ref[...])
    lane = lax.broadcasted_iota(jnp.int32, logits.shape, 1)
    lg = jnp.where(lane < N_EXPERTS, logits, -jnp.inf)
    v1 = jnp.max(lg, axis=-1, keepdims=True)
    i1 = jnp.min(jnp.where(lg == v1, lane, LANES), axis=-1, keepdims=True)
    lg2 = jnp.where(lane == i1, -jnp.inf, lg)
    v2 = jnp.max(lg2, axis=-1, keepdims=True)
    i2 = jnp.min(jnp.where(lg2 == v2, lane, LANES), axis=-1, keepdims=True)
    e2 = jnp.exp(v2 - v1)
    tot = 1.0 + e2
    wt_ref[...] = jnp.where(lane == 0, 1.0 / tot, jnp.where(lane == 1, e2 / tot, 0.0))
    oh1 = (lane == i1).astype(F32)
    oh2 = (lane == i2).astype(F32)
    r = lax.broadcasted_iota(jnp.int32, (TR, TR), 0)
    c = lax.broadcasted_iota(jnp.int32, (TR, TR), 1)
    earlier = (r > c).astype(BF16)
    base = carry_ref[0:1, :]
    c1 = jnp.sum(oh1, axis=0, keepdims=True)
    c2 = jnp.sum(oh2, axis=0, keepdims=True)
    r1 = jnp.sum(oh1 * (base + _dot(earlier, oh1.astype(BF16))), axis=-1, keepdims=True)
    r2 = jnp.sum(oh2 * (base + c1 + _dot(earlier, oh2.astype(BF16))), axis=-1, keepdims=True)
    total = jnp.broadcast_to(base + c1 + c2, carry_ref.shape)
    carry_ref[...] = total
    cnt_ref[...] = total
    meta_ref[...] = jnp.where(lane == 0, i1, jnp.where(lane == 1, i2, jnp.where(
        lane == 2, r1.astype(jnp.int32), jnp.where(lane == 3, r2.astype(jnp.int32), 0))))


def _norm_router(x, g, mods, w_hi, w_lo):
    rows = lambda c: pl.BlockSpec((TR, c), lambda i: (i, 0))
    full = lambda shape: pl.BlockSpec(shape, lambda i: (0, 0))
    mod = lambda k: pl.BlockSpec((None, 1, D_MODEL), lambda i: (_cond_row(i, TR), 0, k))
    return pl.pallas_call(
        _norm_router_kernel,
        grid=(T // TR,),
        in_specs=[rows(D_MODEL), full((1, D_MODEL)), mod(3), mod(4), full((D_MODEL, LANES)), full((D_MODEL, LANES))],
        out_specs=[rows(D_MODEL), rows(LANES), rows(LANES), full((8, LANES))],
        out_shape=[jax.ShapeDtypeStruct((T, D_MODEL), F32),
                   jax.ShapeDtypeStruct((T, LANES), jnp.int32),
                   jax.ShapeDtypeStruct((T, LANES), F32),
                   jax.ShapeDtypeStruct((8, LANES), F32)],
        scratch_shapes=[pltpu.VMEM((8, LANES), F32)],
        compiler_params=_cparams(("arbitrary",)),
        name="norm_router",
    )(x, g.reshape(1, D_MODEL), mods, mods, w_hi, w_lo)


def _dispatch(meta, counts):
    experts = jnp.arange(N_EXPERTS, dtype=jnp.int32)
    cnt = counts[0, :N_EXPERTS].astype(jnp.int32)
    padded = ((cnt + TM_FFN - 1) // TM_FFN) * TM_FFN
    g_end = jnp.sum(jnp.where(experts[None, :] <= experts[:, None], padded[None, :], 0), axis=1)
    g_start = g_end - padded
    e = meta[:, :TOP_K]
    start_of = jnp.sum(jnp.where(e[:, :, None] == experts[None, None, :], g_start[None, None, :], 0), axis=-1)
    slot = start_of + meta[:, TOP_K:2 * TOP_K]
    end_tiles = g_end // TM_FFN
    n_used = end_tiles[N_EXPERTS - 1]
    tiles = jnp.arange(MOE_TILES, dtype=jnp.int32)
    owner = lambda t: jnp.minimum(jnp.sum((end_tiles[None, :] <= t[:, None]).astype(jnp.int32), axis=1), N_EXPERTS - 1)
    tile_e = owner(jnp.minimum(tiles, n_used - 1))
    return slot, g_end, padded, tile_e, n_used.reshape(1)


def _row_copy(src, src_row, dst, dst_row, sem):
    return pltpu.make_async_copy(src.at[pl.ds(src_row, 1), :], dst.at[pl.ds(dst_row, 1), :], sem)


def _moe_scatter_kernel(ge_ref, pd_ref, slot_hbm, h_ref, xs_hbm, slot_smem, zero_ref, sem_ids, sem_rows, sem_zero):
    i = pl.program_id(0)

    def clear_copies(first_row):
        base = pl.multiple_of(first_row, ZR)
        return [pltpu.make_async_copy(zero_ref, xs_hbm.at[pl.ds(base + k * ZR, ZR), :], sem_zero)
                for k in range(TM_FFN // ZR)]

    def for_unused_tiles(fn):
        def body(t, carry):
            for cp in clear_copies(t * TM_FFN):
                fn(cp)
            return carry
        lax.fori_loop(ge_ref[N_EXPERTS - 1] // TM_FFN, MOE_TILES, body, 0)

    @pl.when(i == 0)
    def _():
        zero_ref[...] = jnp.zeros_like(zero_ref)
        for e in range(N_EXPERTS):
            @pl.when(pd_ref[e] > 0)
            def _():
                for cp in clear_copies(ge_ref[e] - TM_FFN):
                    cp.start()
        for_unused_tiles(lambda cp: cp.start())
        for e in range(N_EXPERTS):
            @pl.when(pd_ref[e] > 0)
            def _():
                for cp in clear_copies(ge_ref[e] - TM_FFN):
                    cp.wait()
        for_unused_tiles(lambda cp: cp.wait())

    ids_copy = pltpu.make_async_copy(slot_hbm.at[i], slot_smem, sem_ids)
    ids_copy.start()
    ids_copy.wait()

    def start(r, carry):
        _row_copy(h_ref, r, xs_hbm, slot_smem[TOP_K * r], sem_rows).start(priority=0)
        _row_copy(h_ref, r, xs_hbm, slot_smem[TOP_K * r + 1], sem_rows).start(priority=1)
        return carry

    lax.fori_loop(0, TR, start, 0)

    def wait(r, carry):
        _row_copy(h_ref, r, xs_hbm, 0, sem_rows).wait()
        _row_copy(h_ref, r, xs_hbm, 0, sem_rows).wait()
        return carry

    lax.fori_loop(0, TR, wait, 0)


def _moe_scatter(h, slot, g_end, padded):
    grid_spec = pltpu.PrefetchScalarGridSpec(
        num_scalar_prefetch=2,
        grid=(T // TR,),
        in_specs=[pl.BlockSpec(memory_space=pl.ANY),
                  pl.BlockSpec((TR, D_MODEL), lambda i, ge, pd: (i, 0))],
        out_specs=pl.BlockSpec(memory_space=pl.ANY),
        scratch_shapes=[pltpu.SMEM((TOP_K * TR,), jnp.int32),
                        pltpu.VMEM((ZR, D_MODEL), F32),
                        pltpu.SemaphoreType.DMA,
                        pltpu.SemaphoreType.DMA,
                        pltpu.SemaphoreType.DMA],
    )
    return pl.pallas_call(
        _moe_scatter_kernel,
        grid_spec=grid_spec,
        out_shape=jax.ShapeDtypeStruct((MOE_TILES * TM_FFN, D_MODEL), F32),
        compiler_params=_cparams(("arbitrary",)),
        name="moe_scatter",
    )(g_end, padded, slot.reshape(T // TR, TOP_K * TR), h)


def _ffn_moe_kernel(te_ref, nu_ref, xs_hbm, wg_ref, wu_ref, wd_ref, o_ref, xg_ref, xb_ref, sem):
    i = pl.program_id(0)
    j = pl.program_id(1)
    n_used = nu_ref[0]
    used = i < n_used

    def tile_copy(t):
        return pltpu.make_async_copy(xs_hbm.at[pl.ds(pl.multiple_of(t * TM_FFN, TM_FFN), TM_FFN), :], xg_ref, sem)

    @pl.when(used & (j == 0))
    def _():
        @pl.when(i == 0)
        def _():
            tile_copy(0).start()

        tile_copy(i).wait()
        xb_ref[...] = xg_ref[...].astype(BF16)

        @pl.when(i + 1 < n_used)
        def _():
            tile_copy(i + 1).start()

    @pl.when(used)
    def _():
        _swiglu_accumulate(xb_ref[...], wg_ref, wu_ref, wd_ref, o_ref, j)

    @pl.when(jnp.logical_not(used) & (j == 0))
    def _():
        o_ref[...] = jnp.zeros_like(o_ref)


def _ffn_moe(xs, tile_e, n_used, wg, wu, wd):
    nf = D_FF // TF_MOE

    def hid(i, j, te, nu):
        return jnp.where(i < nu[0], j, nf - 1)

    grid_spec = pltpu.PrefetchScalarGridSpec(
        num_scalar_prefetch=2,
        grid=(MOE_TILES, nf),
        in_specs=[pl.BlockSpec(memory_space=pl.ANY),
                  pl.BlockSpec((None, None, D_MODEL, TF_MOE), lambda i, j, te, nu: (0, te[i], 0, hid(i, j, te, nu))),
                  pl.BlockSpec((None, None, D_MODEL, TF_MOE), lambda i, j, te, nu: (0, te[i], 0, hid(i, j, te, nu))),
                  pl.BlockSpec((None, None, TF_MOE, D_MODEL), lambda i, j, te, nu: (0, te[i], hid(i, j, te, nu), 0))],
        out_specs=pl.BlockSpec((TM_FFN, D_MODEL), lambda i, j, te, nu: (i, 0)),
        scratch_shapes=[pltpu.VMEM((TM_FFN, D_MODEL), F32),
                        pltpu.VMEM((TM_FFN, D_MODEL), BF16),
                        pltpu.SemaphoreType.DMA],
    )
    return pl.pallas_call(
        _ffn_moe_kernel,
        grid_spec=grid_spec,
        out_shape=jax.ShapeDtypeStruct((MOE_TILES * TM_FFN, D_MODEL), F32),
        compiler_params=_cparams(("arbitrary", "arbitrary")),
        name="ffn_moe",
    )(tile_e, n_used, xs, wg, wu, wd)


def _combine_kernel(slot_hbm, ys_hbm, wt_ref, x_ref, gate_ref, g_ref, yp_ref, ysm_ref,
                    slot_smem, y0_ref, y1_ref, sem_ids, sem_rows, *, n_prompt_tiles):
    i = pl.program_id(0)

    def fetch(t):
        b = t % 2
        ids_copy = pltpu.make_async_copy(slot_hbm.at[t], slot_smem.at[b], sem_ids)
        ids_copy.start()
        ids_copy.wait()

        def start(r, carry):
            _row_copy(ys_hbm, slot_smem[b, TOP_K * r], y0_ref.at[b], r, sem_rows.at[b]).start(priority=0)
            _row_copy(ys_hbm, slot_smem[b, TOP_K * r + 1], y1_ref.at[b], r, sem_rows.at[b]).start(priority=1)
            return carry

        lax.fori_loop(0, TC, start, 0)

    @pl.when(i == 0)
    def _():
        fetch(0)

    @pl.when(i + 1 < pl.num_programs(0))
    def _():
        fetch(i + 1)

    b = i % 2

    def wait(r, carry):
        _row_copy(ys_hbm, 0, y0_ref.at[b], r, sem_rows.at[b]).wait()
        _row_copy(ys_hbm, 0, y1_ref.at[b], r, sem_rows.at[b]).wait()
        return carry

    lax.fori_loop(0, TC, wait, 0)
    wt = wt_ref[...]
    y = wt[:, 0:1] * y0_ref[b] + wt[:, 1:2] * y1_ref[b]
    out = _rms(x_ref[...] + gate_ref[...] * y, g_ref[...])

    @pl.when(i < n_prompt_tiles)
    def _():
        yp_ref[...] = out

    @pl.when(i >= n_prompt_tiles)
    def _():
        ysm_ref[...] = out


def _combine(slot, ys, wt, x, mods, k_gate, g):
    n_p = TP // TC
    return pl.pallas_call(
        functools.partial(_combine_kernel, n_prompt_tiles=n_p),
        grid=(T // TC,),
        in_specs=[pl.BlockSpec(memory_space=pl.ANY),
                  pl.BlockSpec(memory_space=pl.ANY),
                  pl.BlockSpec((TC, LANES), lambda i: (i, 0)),
                  pl.BlockSpec((TC, D_MODEL), lambda i: (i, 0)),
                  pl.BlockSpec((None, 1, D_MODEL), lambda i: (_cond_row(i, TC), 0, k_gate)),
                  pl.BlockSpec((1, D_MODEL), lambda i: (0, 0))],
        out_specs=[pl.BlockSpec((TC, D_MODEL), lambda i: (jnp.minimum(i, n_p - 1), 0)),
                   pl.BlockSpec((TC, D_MODEL), lambda i: (jnp.maximum(i - n_p, 0), 0))],
        out_shape=[jax.ShapeDtypeStruct((TP, D_MODEL), F32), jax.ShapeDtypeStruct((TS, D_MODEL), F32)],
        scratch_shapes=[pltpu.SMEM((2, TOP_K * TC), jnp.int32),
                        pltpu.VMEM((2, TC, D_MODEL), F32),
                        pltpu.VMEM((2, TC, D_MODEL), F32),
                        pltpu.SemaphoreType.DMA,
                        pltpu.SemaphoreType.DMA((2,))],
        compiler_params=_cparams(("arbitrary",)),
        name="moe_combine",
    )(slot.reshape(T // TC, TOP_K * TC), ys, wt, x, mods, g.reshape(1, D_MODEL))


def kernel(x_prompt, x_sample, cache_attn_k, cache_attn_v, cache_mla_ckv, cache_mla_krope, c, c_ctx, ln1_g, ln2_g, w_ada, b_ada, w_in, attn_sink, mla_q_norm_g, w_uq, mla_kv_norm_g, w_ukv, pool_w, pool_scale, w_branch_a, w_branch_b, w_branch_c, w_out, ffn_w_gate, ffn_w_up, ffn_w_down, router_w, moe_w_gate, moe_w_up, moe_w_down, final_g):
    x = (x_prompt.reshape(TP, D_MODEL), x_sample.reshape(TS, D_MODEL))
    w_in_t = jnp.swapaxes(w_in, 1, 2)
    cond =jnp.concatenate([c_ctx[None, :], c, jnp.zeros((N_COND - 1 - DEC_BATCH, D_MODEL), F32)], axis=0)
    rows = DEC_SEQ // GRID_W
    cos_a, sin_a = _rope_tables(rows, HD_A)
    cos_b, sin_b = _rope_tables(rows, QK_ROPE)
    st_k, st_v, st_ckv, st_kr = [], [], [], []
    assert DEPTH == 2, "layer 0 is the dense-FFN layer, layer 1 the expert layer that ends the trunk"
    all_mods = [_adaln(cond, w_ada, b_ada, l).reshape(N_COND, 1, 6 * D_MODEL) for l in range(DEPTH)]
    cos_t = jnp.concatenate([jnp.ones((TM, LANES), F32), cos_b], axis=0)
    sin_t = jnp.concatenate([jnp.zeros((TM, LANES), F32), sin_b], axis=0)
    h1 = _norm_mod(x[0], x[1], ln1_g[0], all_mods[0], 0, 1)
    for l in range(DEPTH):
        mods = all_mods[l]
        qa = _matmul_wt(h1, w_in_t, l, 0, A_W, A_W, F32, "proj_q")
        kv = _matmul_wt(h1, w_in_t, l, OFF_K, 2 * KV_W, 2 * KV_W, F32, "proj_kv")
        zb = _matmul_wt(h1, w_in_t, l, OFF_CQ, OFF_POOL - OFF_CQ, OFF_POOL - OFF_CQ, F32, "proj_mla")
        u = _matmul_wt(h1, w_in_t, l, OFF_POOL, C_W, C_W, F32, "proj_pool")
        gates = _matmul_wt(h1, w_in_t, l, OFF_GATE, 3 * D_MODEL, 1024, BF16, "proj_gates")
        sink = attn_sink[l]
        ck = cache_attn_k[:, l].reshape(DEC_BATCH, PAST_LEN, KV_W)
        cv = cache_attn_v[:, l].reshape(DEC_BATCH, PAST_LEN, KV_W)
        oa = (_attn_a_prompt(qa, kv, sink), _attn_a_sample(qa, kv, ck, cv, sink, cos_a, sin_a))
        wq_cat, w_aug = _mla_weights(w_uq[l], w_ukv[l])
        q, ckv, kr, xa = _mla_prep(zb, mla_q_norm_g[l], mla_kv_norm_g[l], wq_cat, cos_t, sin_t)
        cache_aug = jnp.concatenate([cache_mla_ckv[:, l], cache_mla_krope[:, l]], axis=-1).astype(BF16)
        xa_s = jnp.concatenate([xa[TP:].reshape(DEC_BATCH, DEC_SEQ, MLA_AUG), cache_aug], axis=1)
        kv_p = _matmul(xa[:TP], w_aug, BF16, TM, B_W, "mla_kv_prompt")
        kv_s = _matmul(xa_s.reshape(DEC_BATCH * MLA_KEYS, MLA_AUG), w_aug, BF16, TM, B_W, "mla_kv_sample")
        ob = (_mla_attend(q, kv_p, SEQ, 0, BATCH, 1, SEQ, "mla_prompt"),
              _mla_attend(q, kv_s, TQ_MLA, TP // TQ_MLA, DEC_BATCH, DEC_SEQ // TQ_MLA, MLA_KEYS, "mla_sample"))
        pw = pool_w[l].astype(BF16)
        oc = (_pool(u, pw, pool_scale[l], SEQ, BATCH, 0, "pool_prompt"),
              _pool(u, pw, pool_scale[l], DEC_SEQ, DEC_BATCH, TP // DEC_SEQ, "pool_sample"))
        y = _merge((oa[0], ob[0], oc[0]), (oa[1], ob[1], oc[1]), gates, w_branch_a[l].astype(BF16),
                   w_branch_b[l].astype(BF16), w_branch_c[l].astype(BF16))
        st_k.append(kv[:TP, :KV_W].reshape(BATCH, SEQ, N_KV_A, HD_A))
        st_v.append(kv[:TP, KV_W:].reshape(BATCH, SEQ, N_KV_A, HD_A))
        st_ckv.append(ckv[:TP].reshape(BATCH, SEQ, KV_LORA))
        st_kr.append(kr[:TP].reshape(BATCH, SEQ, QK_ROPE))
        if l == 0:
            x, h2 = _out_proj(y, w_out[l].astype(BF16), x, mods, ln2_g[l], True)
            f = _ffn_dense(h2, ffn_w_gate[0].astype(BF16), ffn_w_up[0].astype(BF16), ffn_w_down[0].astype(BF16))
            x, h1 = _resid_norm(x, f, mods, 5, ln1_g[1], all_mods[1], 0, 1)
        else:
            (x,) = _out_proj(y, w_out[l].astype(BF16), x, mods, ln2_g[l], False)
            rw = jnp.pad(router_w[0], ((0, 0), (0, LANES - N_EXPERTS)))
            rw_hi = rw.astype(BF16)
            rw_lo = (rw - rw_hi.astype(F32)).astype(BF16)
            h2, meta, wt, counts = _norm_router(x, ln2_g[l], mods, rw_hi, rw_lo)
            slot, g_end, padded, tile_e, n_used = _dispatch(meta, counts)
            xs = _moe_scatter(h2, slot, g_end, padded)
            ys = _ffn_moe(xs, tile_e, n_used, moe_w_gate, moe_w_up, moe_w_down)
            y_p, y_s = _combine(slot, ys, wt, x, mods, 5, final_g)
    return (y_p.reshape(BATCH, SEQ, D_MODEL), y_s.reshape(DEC_BATCH, DEC_SEQ, D_MODEL),
            jnp.stack(st_k, axis=1), jnp.stack(st_v, axis=1), jnp.stack(st_ckv, axis=1), jnp.stack(st_kr, axis=1))
```

```python
import functools
import math

import jax
import jax.numpy as jnp
from jax import lax
from jax.experimental import pallas as pl
from jax.experimental.pallas import tpu as pltpu

BF16 = jnp.bfloat16
F32 = jnp.float32

D_MODEL = 2048
BATCH = 16
SEQ = 256
DEPTH = 2
DEC_BATCH = 8
DEC_SEQ = 2048
PAST_LEN = 512
GRID_W = 64
BLK = 128
EPS = 1e-6
ROPE_BASE = 10000.0
NEG_INF = -1e30
N_HEADS_A = 16
N_KV_A = 2
HD_A = 64
WINDOW = 128
A_W = N_HEADS_A * HD_A
KV_W = N_KV_A * HD_A
N_HEADS_B = 8
Q_LORA = 512
KV_LORA = 256
QK_NOPE = 128
QK_ROPE = 64
V_HD = 128
B_W = N_HEADS_B * V_HD
POOL_WINDOWS = (2, 4, 8, 16)
POOL_GROUPS = 4
POOL_GW = 256
C_W = POOL_GROUPS * POOL_GW
OFF_K = A_W
OFF_V = OFF_K + KV_W
OFF_CQ = OFF_V + KV_W
OFF_CKV = OFF_CQ + Q_LORA
OFF_KR = OFF_CKV + KV_LORA
OFF_POOL = OFF_KR + QK_ROPE
OFF_GATE = OFF_POOL + C_W
IN_COLS = OFF_GATE + 3 * D_MODEL
D_FF = 5632
N_EXPERTS = 8
TOP_K = 2

TP = BATCH * SEQ
TS = DEC_BATCH * DEC_SEQ
T = TP + TS
N_COND = 16
MLA_KEYS = DEC_SEQ + PAST_LEN
LANES = 128
POOL_HALO = 8
LOG2E = math.log2(math.e)
MLA_HEAD_K = 256
MLA_AUG = KV_LORA + QK_ROPE

VMEM_LIMIT = 56 * 1024 * 1024

TM = 1024
TM_FFN = 1024
TF = 512
NF = D_FF // TF
TF_MOE = 256
MOE_TILES = (TOP_K * T) // TM_FFN + N_EXPERTS
TR = 512
ZR = 256
TC = 256
TQ_MLA = 512
TQ_A = 256


def _cparams(sem):
    return pltpu.CompilerParams(dimension_semantics=sem, vmem_limit_bytes=VMEM_LIMIT)


def _cond_row(i, tm):
    n_p = TP // tm
    per_b = DEC_SEQ // tm
    return jnp.where(i < n_p, 0, (i - n_p) // per_b + 1)


def _dot(a, b):
    return jnp.dot(a, b, preferred_element_type=F32)


def _pack_bf16_pairs(x):
    n = x.shape[1] // 2
    hi = lax.bitcast_convert_type(x[:, :n].astype(F32), jnp.uint32)
    lo = lax.bitcast_convert_type(x[:, n:].astype(F32), jnp.uint32)
    return hi | (lo >> 16)


def _unpack_bf16_pairs(w, o_ref):
    n = w.shape[1]
    o_ref[:, :n] = lax.bitcast_convert_type(w & jnp.uint32(0xFFFF0000), F32).astype(BF16)
    o_ref[:, n:] = lax.bitcast_convert_type(w << 16, F32).astype(BF16)


def _dot_nt(a, b):
    return lax.dot_general(a, b, (((1,), (1,)), ((), ())), preferred_element_type=F32)


def _mm_kernel(x_ref, w_ref, o_ref):
    o_ref[...] = _dot(x_ref[...].astype(BF16), w_ref[...].astype(BF16)).astype(o_ref.dtype)


def _matmul(x, w, out_dtype, tm, tn, name):
    m, k = x.shape
    n = w.shape[1]
    return pl.pallas_call(
        _mm_kernel,
        grid=(m // tm, n // tn),
        in_specs=[pl.BlockSpec((tm, k), lambda i, j: (i, 0)),
                  pl.BlockSpec((k, tn), lambda i, j: (0, j))],
        out_specs=pl.BlockSpec((tm, tn), lambda i, j: (i, j)),
        out_shape=jax.ShapeDtypeStruct((m, n), out_dtype),
        compiler_params=_cparams(("parallel", "arbitrary")),
        name=name,
    )(x, w)


def _mm_wt_kernel(x_ref, wt_hbm, o_ref, stage_ref, wb_ref, sem, *, layer, row0, tn):
    j = pl.program_id(0)

    def tile_copy(t):
        first = pl.multiple_of(row0 + t * tn, 8)
        return pltpu.make_async_copy(wt_hbm.at[layer, pl.ds(first, tn), :], stage_ref, sem)

    @pl.when(pl.program_id(1) == 0)
    def _():
        @pl.when(j == 0)
        def _():
            tile_copy(0).start()

        tile_copy(j).wait()
        wb_ref[...] = stage_ref[...].astype(BF16)

        @pl.when(j + 1 < pl.num_programs(0))
        def _():
            tile_copy(j + 1).start()

    o_ref[...] = _dot_nt(x_ref[...], wb_ref[...]).astype(o_ref.dtype)


def _matmul_wt(x, wt, layer, row0, n, tn, out_dtype, name):
    m, k = x.shape
    tm = TM
    assert n % tn == 0 and row0 % 8 == 0 and tn % 16 == 0
    return pl.pallas_call(
        functools.partial(_mm_wt_kernel, layer=layer, row0=row0, tn=tn),
        grid=(n // tn, m // tm),
        in_specs=[pl.BlockSpec((tm, k), lambda j, i: (i, 0)), pl.BlockSpec(memory_space=pl.ANY)],
        out_specs=pl.BlockSpec((tm, tn), lambda j, i: (i, j)),
        out_shape=jax.ShapeDtypeStruct((m, n), out_dtype),
        scratch_shapes=[pltpu.VMEM((tn, k), F32), pltpu.VMEM((tn, k), BF16), pltpu.SemaphoreType.DMA],
        compiler_params=_cparams(("arbitrary", "arbitrary")),
        name=name,
    )(x, wt)


def _ada_kernel(c_ref, w_ref, b_ref, o_ref):
    c = c_ref[...]
    a = (c * jax.nn.sigmoid(c)).astype(BF16)
    o_ref[...] = _dot(a, w_ref[...].astype(BF16)) + b_ref[...]


def _adaln(cond, w, b, layer):
    n = w.shape[2]
    tn = 1024
    return pl.pallas_call(
        _ada_kernel,
        grid=(n // tn,),
        in_specs=[pl.BlockSpec((N_COND, D_MODEL), lambda j: (0, 0)),
                  pl.BlockSpec((None, D_MODEL, tn), lambda j: (layer, 0, j)),
                  pl.BlockSpec((None, 1, tn), lambda j: (layer, 0, j))],
        out_specs=pl.BlockSpec((N_COND, tn), lambda j: (0, j)),
        out_shape=jax.ShapeDtypeStruct((N_COND, n), F32),
        compiler_params=_cparams(("arbitrary",)),
        name="adaln",
    )(cond, w, b.reshape(DEPTH, 1, n))


def _split_row_specs(tm, width):
    n_p = TP // tm
    return [pl.BlockSpec((tm, width), lambda i, *_: (jnp.minimum(i, n_p - 1), 0)),
            pl.BlockSpec((tm, width), lambda i, *_: (jnp.maximum(i - n_p, 0), 0))], n_p


def _pick_rows(p_ref, s_ref, n_prompt_tiles):
    return jnp.where(pl.program_id(0) < n_prompt_tiles, p_ref[...], s_ref[...])


def _norm_mod_kernel(xp_ref, xs_ref, g_ref, sh_ref, sc_ref, o_ref, *, n_prompt_tiles):
    x = _pick_rows(xp_ref, xs_ref, n_prompt_tiles)
    o_ref[...] = (_rms(x, g_ref[...]) * (1 + sc_ref[...]) + sh_ref[...]).astype(o_ref.dtype)


def _norm_mod(xp, xs, g, mods, k_shift, k_scale):
    tm = 512
    x_specs, n_p = _split_row_specs(tm, D_MODEL)
    return pl.pallas_call(
        functools.partial(_norm_mod_kernel, n_prompt_tiles=n_p),
        grid=(T // tm,),
        in_specs=x_specs + [pl.BlockSpec((1, D_MODEL), lambda i: (0, 0)),
                            pl.BlockSpec((None, 1, D_MODEL), lambda i: (_cond_row(i, tm), 0, k_shift)),
                            pl.BlockSpec((None, 1, D_MODEL), lambda i: (_cond_row(i, tm), 0, k_scale))],
        out_specs=pl.BlockSpec((tm, D_MODEL), lambda i: (i, 0)),
        out_shape=jax.ShapeDtypeStruct((T, D_MODEL), BF16),
        compiler_params=_cparams(("parallel",)),
        name="norm_mod1",
    )(xp, xs, g.reshape(1, D_MODEL), mods, mods)


def _rms(x, g):
    return x * lax.rsqrt(jnp.mean(x * x, axis=-1, keepdims=True) + EPS) * g


def _resid_norm_kernel(x_ref, f_ref, gate_ref, g_ref, sh_ref, sc_ref, x2_ref, h_ref):
    x2 = x_ref[...] + gate_ref[...] * f_ref[...]
    x2_ref[...] = x2
    h_ref[...] = (_rms(x2, g_ref[...]) * (1 + sc_ref[...]) + sh_ref[...]).astype(h_ref.dtype)


def _resid_norm(x, f, mods, k_gate, g, mods_next, k_shift, k_scale):
    tm = 512
    rows = pl.BlockSpec((tm, D_MODEL), lambda i: (i, 0))
    mod = lambda k: pl.BlockSpec((None, 1, D_MODEL), lambda i: (_cond_row(i, tm), 0, k))
    return pl.pallas_call(
        _resid_norm_kernel,
        grid=(T // tm,),
        in_specs=[rows, rows, mod(k_gate), pl.BlockSpec((1, D_MODEL), lambda i: (0, 0)), mod(k_shift), mod(k_scale)],
        out_specs=[rows, rows],
        out_shape=[jax.ShapeDtypeStruct((T, D_MODEL), F32), jax.ShapeDtypeStruct((T, D_MODEL), BF16)],
        compiler_params=_cparams(("parallel",)),
        name="resid_norm",
    )(x, f, mods, g.reshape(1, D_MODEL), mods_next, mods_next)


def _rope_tables(rows, dim):
    quarter = dim // 4
    inv = ROPE_BASE ** (-jnp.arange(quarter, dtype=F32) / quarter)
    r = jnp.repeat(jnp.arange(rows, dtype=F32), GRID_W)
    col = jnp.tile(jnp.arange(GRID_W, dtype=F32), rows)
    ar, ac = r[:, None] * inv, col[:, None] * inv
    cos = jnp.concatenate([jnp.cos(ar), jnp.cos(ar), jnp.cos(ac), jnp.cos(ac)], axis=-1)
    sin = jnp.concatenate([-jnp.sin(ar), jnp.sin(ar), -jnp.sin(ac), jnp.sin(ac)], axis=-1)
    reps = LANES // dim
    return jnp.tile(cos, (1, reps)), jnp.tile(sin, (1, reps))


def _rope_lanes(x, cos, sin, quarter):
    lane = lax.broadcasted_iota(jnp.int32, (x.shape[0], LANES), 1)
    first = (lane % (2 * quarter)) < quarter
    outs = []
    for c in range(x.shape[1] // LANES):
        xc = x[:, c * LANES:(c + 1) * LANES]
        partner = jnp.where(first, pltpu.roll(xc, LANES - quarter, 1), pltpu.roll(xc, quarter, 1))
        outs.append(xc * cos + partner * sin)
    return outs[0] if len(outs) == 1 else jnp.concatenate(outs, axis=1)


def _attn_a_kernel(*refs, local, tq, n_blocks):
    assert N_KV_A * HD_A == LANES
    if local:
        (sink_ref, q_ref, kc_ref, vc_ref, kvm_ref, kv0_ref, kvp_ref,
         cq_ref, sq_ref, ckm_ref, skm_ref, ck0_ref, sk0_ref, ckp_ref, skp_ref, o_ref) = refs
    else:
        sink_ref, q_ref, kc_ref, vc_ref, o_ref = refs
    quarter = HD_A // 4
    q = q_ref[...]
    k_all = kc_ref[...]
    v_all = vc_ref[...]
    n_loc = 0
    if local:
        j = pl.program_id(1)
        n_loc = tq + 2 * BLK
        q = _rope_lanes(q, cq_ref[...], sq_ref[...], quarter)
        k_all = jnp.concatenate([
            _rope_lanes(kvm_ref[:, :KV_W], ckm_ref[...], skm_ref[...], quarter),
            _rope_lanes(kv0_ref[:, :KV_W], ck0_ref[...], sk0_ref[...], quarter),
            _rope_lanes(kvp_ref[:, :KV_W], ckp_ref[...], skp_ref[...], quarter), k_all], axis=0)
        v_all = jnp.concatenate([kvm_ref[:, KV_W:], kv0_ref[:, KV_W:], kvp_ref[:, KV_W:], v_all], axis=0)
        qi = lax.broadcasted_iota(jnp.int32, (tq, n_loc), 0)
        ki = lax.broadcasted_iota(jnp.int32, (tq, n_loc), 1)
        k_lo = jnp.where(j == 0, BLK, 0)
        k_hi = jnp.where(j == n_blocks - 1, n_loc - BLK, n_loc)
        valid = (ki >= qi) & (ki <= qi + 2 * WINDOW) & (ki >= k_lo) & (ki < k_hi)
        bias = jnp.where(valid, 0.0, NEG_INF)
    q = (q * (HD_A ** -0.5 * LOG2E)).astype(BF16)
    nk = k_all.shape[0]
    low_k = lax.broadcasted_iota(jnp.int32, (nk, LANES), 1) < HD_A
    low_q = lax.broadcasted_iota(jnp.int32, (tq, LANES), 1) < HD_A
    rep = N_HEADS_A // N_KV_A

    def blockdiag(x, g):
        swapped = pltpu.roll(x, HD_A, 1)
        lo_src, hi_src = (x, swapped) if g == 0 else (swapped, x)
        return jnp.concatenate([jnp.where(low_k, lo_src, 0.0), jnp.where(low_k, 0.0, hi_src)], axis=0).astype(BF16)

    for g in range(N_KV_A):
        k2 = blockdiag(k_all, g)
        v2 = blockdiag(v_all, g)
        for p in range(rep // 2):
            h0 = g * rep + 2 * p
            s = _dot_nt(q[:, h0 * HD_A:h0 * HD_A + LANES], k2)
            parts, inv = [], []
            for t in range(2):
                sh = s[:, t * nk:(t + 1) * nk]
                sk = sink_ref[h0 + t] * LOG2E
                if local:
                    s_loc = sh[:, :n_loc] + bias
                    s_ctx = sh[:, n_loc:]
                    m = jnp.maximum(jnp.maximum(jnp.max(s_loc, axis=-1, keepdims=True),
                                                jnp.max(s_ctx, axis=-1, keepdims=True)), sk)
                    p_loc = jnp.exp2(s_loc - m)
                    p_ctx = jnp.exp2(s_ctx - m)
                    denom = (jnp.sum(p_loc, axis=-1, keepdims=True) + jnp.sum(p_ctx, axis=-1, keepdims=True)
                             + jnp.exp2(sk - m))
                    parts += [p_loc, p_ctx]
                else:
                    m = jnp.maximum(jnp.max(sh, axis=-1, keepdims=True), sk)
                    p_all = jnp.exp2(sh - m)
                    denom = jnp.sum(p_all, axis=-1, keepdims=True) + jnp.exp2(sk - m)
                    parts.append(p_all)
                inv.append(1.0 / denom)
            o2 = _dot(jnp.concatenate(parts, axis=1).astype(BF16), v2) * jnp.where(low_q, inv[0], inv[1])
            o_ref[:, h0 * HD_A:h0 * HD_A + LANES] = o2.astype(o_ref.dtype)


def _attn_a_prompt(qa, kv, sink):
    kern = functools.partial(_attn_a_kernel, local=False, tq=SEQ, n_blocks=1)
    return pl.pallas_call(
        kern,
        grid=(BATCH,),
        in_specs=[pl.BlockSpec(memory_space=pltpu.SMEM),
                  pl.BlockSpec((SEQ, A_W), lambda b: (b, 0)),
                  pl.BlockSpec((SEQ, KV_W), lambda b: (b, 0)),
                  pl.BlockSpec((SEQ, KV_W), lambda b: (b, 1))],
        out_specs=pl.BlockSpec((SEQ, A_W), lambda b: (b, 0)),
        out_shape=jax.ShapeDtypeStruct((TP, A_W), BF16),
        compiler_params=_cparams(("parallel",)),
        name="attn_a_prompt",
    )(sink, qa, kv, kv)


def _attn_a_sample(qa, kv, ck, cv, sink, cos, sin):
    tq = TQ_A
    qb = tq // BLK
    nb = DEC_SEQ // BLK
    nq = DEC_SEQ // tq
    base_blk = TP // BLK
    base_q = TP // tq

    def prev_blk(j):
        return jnp.maximum(qb * j - 1, 0)

    def next_blk(j):
        return jnp.minimum(qb * j + qb, nb - 1)

    edge_kv = lambda f: pl.BlockSpec((BLK, 2 * KV_W), lambda b, j: (base_blk + b * nb + f(j), 0))
    edge_tab = lambda f: pl.BlockSpec((BLK, LANES), lambda b, j: (f(j), 0))
    own_tab = pl.BlockSpec((tq, LANES), lambda b, j: (j, 0))
    kern = functools.partial(_attn_a_kernel, local=True, tq=tq, n_blocks=nq)
    return pl.pallas_call(
        kern,
        grid=(DEC_BATCH, nq),
        in_specs=[pl.BlockSpec(memory_space=pltpu.SMEM),
                  pl.BlockSpec((tq, A_W), lambda b, j: (base_q + b * nq + j, 0)),
                  pl.BlockSpec((None, PAST_LEN, KV_W), lambda b, j: (b, 0, 0)),
                  pl.BlockSpec((None, PAST_LEN, KV_W), lambda b, j: (b, 0, 0)),
                  edge_kv(prev_blk),
                  pl.BlockSpec((tq, 2 * KV_W), lambda b, j: (base_q + b * nq + j, 0)),
                  edge_kv(next_blk),
                  own_tab, own_tab, edge_tab(prev_blk), edge_tab(prev_blk),
                  own_tab, own_tab, edge_tab(next_blk), edge_tab(next_blk)],
        out_specs=pl.BlockSpec((tq, A_W), lambda b, j: (b * nq + j, 0)),
        out_shape=jax.ShapeDtypeStruct((TS, A_W), BF16),
        compiler_params=_cparams(("parallel", "arbitrary")),
        name="attn_a_sample",
    )(sink, qa, ck, cv, kv, kv, kv, cos, sin, cos, sin, cos, sin, cos, sin)


def _mla_prep_kernel(z_ref, gq_ref, gkv_ref, wq_ref, cos_ref, sin_ref, q_ref, ckv_ref, kr_ref, xa_ref):
    quarter = QK_ROPE // 4
    c = (QK_NOPE + QK_ROPE) ** -0.5 * LOG2E
    cos = cos_ref[...]
    sin = sin_ref[...]
    cq = z_ref[:, :Q_LORA]
    q = _dot(_rms(cq, gq_ref[...]).astype(BF16), wq_ref[...])
    for h in range(N_HEADS_B):
        lo = h * MLA_HEAD_K
        q_ref[:, lo:lo + QK_NOPE] = (q[:, lo:lo + QK_NOPE] * c).astype(q_ref.dtype)
        rot = _rope_lanes(q[:, lo + QK_NOPE:lo + MLA_HEAD_K], cos, sin, quarter)
        q_ref[:, lo + QK_NOPE:lo + MLA_HEAD_K] = (rot * c).astype(q_ref.dtype)
    ckv = _rms(z_ref[:, Q_LORA:Q_LORA + KV_LORA], gkv_ref[...])
    ckv_ref[...] = ckv
    kr = z_ref[:, Q_LORA + KV_LORA:]
    kr_ref[...] = kr
    xa_ref[:, :KV_LORA] = ckv.astype(xa_ref.dtype)
    kr_rot = _rope_lanes(jnp.concatenate([kr, kr], axis=1), cos, sin, quarter)
    xa_ref[:, KV_LORA:] = kr_rot[:, :QK_ROPE].astype(xa_ref.dtype)


def _mla_prep(z, gq, gkv, wq_cat, cos, sin):
    tm = TM
    zc = Q_LORA + KV_LORA + QK_ROPE
    n_p = TP // tm
    per_b = DEC_SEQ // tm
    full = lambda shape: pl.BlockSpec(shape, lambda i: (0, 0))
    rows = lambda c: pl.BlockSpec((tm, c), lambda i: (i, 0))
    tab = pl.BlockSpec((tm, LANES), lambda i: (jnp.where(i < n_p, 0, 1 + (i - n_p) % per_b), 0))
    return pl.pallas_call(
        _mla_prep_kernel,
        grid=(T // tm,),
        in_specs=[rows(zc), full((1, Q_LORA)), full((1, KV_LORA)), full((Q_LORA, N_HEADS_B * MLA_HEAD_K)), tab, tab],
        out_specs=[rows(N_HEADS_B * MLA_HEAD_K), rows(KV_LORA), rows(QK_ROPE), rows(MLA_AUG)],
        out_shape=[jax.ShapeDtypeStruct((T, N_HEADS_B * MLA_HEAD_K), BF16),
                   jax.ShapeDtypeStruct((T, KV_LORA), F32),
                   jax.ShapeDtypeStruct((T, QK_ROPE), F32),
                   jax.ShapeDtypeStruct((T, MLA_AUG), BF16)],
        compiler_params=_cparams(("parallel",)),
        name="mla_prep",
    )(z, gq.reshape(1, Q_LORA), gkv.reshape(1, KV_LORA), wq_cat, cos, sin)


def _mla_weights(w_uq, w_ukv):
    wq = w_uq.reshape(Q_LORA, N_HEADS_B, QK_NOPE + QK_ROPE)
    wq = jnp.pad(wq, ((0, 0), (0, 0), (0, MLA_HEAD_K - QK_NOPE - QK_ROPE))).reshape(Q_LORA, N_HEADS_B * MLA_HEAD_K)
    wkv = w_ukv.reshape(KV_LORA, N_HEADS_B, QK_NOPE + V_HD)
    wk = jnp.pad(wkv[:, :, :QK_NOPE], ((0, QK_ROPE), (0, 0), (0, MLA_HEAD_K - QK_NOPE)))
    eye = jnp.pad(jnp.eye(QK_ROPE, dtype=F32), ((KV_LORA, 0), (QK_NOPE, MLA_HEAD_K - QK_NOPE - QK_ROPE)))
    wk = (wk + eye[:, None, :]).reshape(MLA_AUG, N_HEADS_B * MLA_HEAD_K)
    wv = jnp.pad(wkv[:, :, QK_NOPE:], ((0, QK_ROPE), (0, 0), (0, 0))).reshape(MLA_AUG, B_W)
    return wq.astype(BF16), jnp.concatenate([wk, wv], axis=1).astype(BF16)


def _mla_kernel(q_ref, k_ref, v_ref, o_ref):
    for h in range(N_HEADS_B):
        ks = slice(h * MLA_HEAD_K, (h + 1) * MLA_HEAD_K)
        vs = slice(h * V_HD, (h + 1) * V_HD)
        s = _dot_nt(q_ref[:, ks], k_ref[:, ks])
        p = jnp.exp2(s - jnp.max(s, axis=-1, keepdims=True))
        denom = jnp.sum(p, axis=-1, keepdims=True)
        o_ref[:, vs] = (_dot(p.astype(BF16), v_ref[:, vs]) / denom).astype(o_ref.dtype)


def _mla_attend(q, kv, tq, q_base, n_seq, nq, nk, name):
    kw = N_HEADS_B * MLA_HEAD_K
    return pl.pallas_call(
        _mla_kernel,
        grid=(n_seq, nq),
        in_specs=[pl.BlockSpec((tq, kw), lambda b, i: (q_base + b * nq + i, 0)),
                  pl.BlockSpec((nk, kw), lambda b, i: (b, 0), pipeline_mode=pl.Buffered(1)),
                  pl.BlockSpec((nk, B_W), lambda b, i: (b, kw // B_W), pipeline_mode=pl.Buffered(1))],
        out_specs=pl.BlockSpec((tq, B_W), lambda b, i: (b * nq + i, 0)),
        out_shape=jax.ShapeDtypeStruct((n_seq * nq * tq, B_W), BF16),
        compiler_params=_cparams(("parallel", "arbitrary")),
        name=name,
    )(q, kv, kv)


def _pool_kernel(u_ref, w_ref, s_ref, o_ref, pad_ref, *, n):
    chunk = min(n, 256)
    zeros = jnp.zeros((POOL_HALO, C_W), F32)
    pad_ref[0:POOL_HALO, :] = zeros
    pad_ref[POOL_HALO + n:POOL_HALO + n + POOL_HALO, :] = zeros
    pad_ref[POOL_HALO:POOL_HALO + n, :] = u_ref[...]
    for c in range(n // chunk):
        r0 = c * chunk
        t = lax.broadcasted_iota(jnp.int32, (chunk, 1), 0) + r0
        for g, win in enumerate(POOL_WINDOWS):
            left = win // 2
            right = win - left - 1
            cols = slice(g * POOL_GW, (g + 1) * POOL_GW)
            acc = pad_ref[POOL_HALO + r0 - left:POOL_HALO + r0 - left + chunk, cols]
            for k in range(-left + 1, right + 1):
                acc = acc + pad_ref[POOL_HALO + r0 + k:POOL_HALO + r0 + k + chunk, cols]
            cnt = (jnp.minimum(t + right, n - 1) + 1 - jnp.maximum(t - left, 0)).astype(F32)
            d = acc / cnt - pad_ref[POOL_HALO + r0:POOL_HALO + r0 + chunk, cols]
            y = _dot(d.astype(BF16), w_ref[g]) * s_ref[:, cols]
            o_ref[r0:r0 + chunk, cols] = y.astype(o_ref.dtype)


def _pool(u, w, scale, n, n_seq, row_base, name):
    kern = functools.partial(_pool_kernel, n=n)
    return pl.pallas_call(
        kern,
        grid=(n_seq,),
        in_specs=[pl.BlockSpec((n, C_W), lambda b: (row_base + b, 0)),
                  pl.BlockSpec((POOL_GROUPS, POOL_GW, POOL_GW), lambda b: (0, 0, 0)),
                  pl.BlockSpec((1, C_W), lambda b: (0, 0))],
        out_specs=pl.BlockSpec((n, C_W), lambda b: (b, 0)),
        out_shape=jax.ShapeDtypeStruct((n_seq * n, C_W), BF16),
        scratch_shapes=[pltpu.VMEM((n + 2 * POOL_HALO, C_W), F32)],
        compiler_params=_cparams(("parallel",)),
        name=name,
    )(u, w, scale.reshape(1, C_W))


def _merge_kernel(oap_ref, obp_ref, ocp_ref, oas_ref, obs_ref, ocs_ref, ga_ref, gb_ref, gc_ref,
                  wa_ref, wb_ref, wc_ref, o_ref, *, n_prompt_tiles):
    is_p = pl.program_id(0) < n_prompt_tiles
    pick = lambda p_ref, s_ref: jnp.where(is_p, p_ref[...], s_ref[...])
    y = jax.nn.sigmoid(ga_ref[...].astype(F32)) * _dot(pick(oap_ref, oas_ref), wa_ref[...])
    y = y + jax.nn.sigmoid(gb_ref[...].astype(F32)) * _dot(pick(obp_ref, obs_ref), wb_ref[...])
    y = y + jax.nn.sigmoid(gc_ref[...].astype(F32)) * _dot(pick(ocp_ref, ocs_ref), wc_ref[...])
    o_ref[...] = y.astype(o_ref.dtype)


def _merge(branches_p, branches_s, gates, wa, wb, wc):
    tm, tn = TM, 1024
    nn = D_MODEL // tn
    n_p = TP // tm
    rows_p = pl.BlockSpec((tm, A_W), lambda i, j: (jnp.minimum(i, n_p - 1), 0), pipeline_mode=pl.Buffered(1))
    rows_s = pl.BlockSpec((tm, A_W), lambda i, j: (jnp.maximum(i - n_p, 0), 0))
    gate = lambda k: pl.BlockSpec((tm, tn), lambda i, j: (i, k * nn + j))
    wspec = pl.BlockSpec((A_W, tn), lambda i, j: (0, j))
    return pl.pallas_call(
        functools.partial(_merge_kernel, n_prompt_tiles=n_p),
        grid=(T // tm, nn),
        in_specs=[rows_p] * 3 + [rows_s] * 3 + [gate(0), gate(1), gate(2), wspec, wspec, wspec],
        out_specs=pl.BlockSpec((tm, tn), lambda i, j: (i, j)),
        out_shape=jax.ShapeDtypeStruct((T, D_MODEL), BF16),
        compiler_params=_cparams(("parallel", "arbitrary")),
        name="merge",
    )(*branches_p, *branches_s, gates, gates, gates, wa, wb, wc)


def _out_proj_kernel(*refs, n_x, n_prompt_tiles):
    y_ref, w_ref = refs[:2]
    x_refs = refs[2:2 + n_x]
    gate_ref, g_ref, sh_ref, sc_ref, x1_ref = refs[2 + n_x:7 + n_x]
    h_refs = refs[7 + n_x:]
    x = x_refs[0][...] if n_x == 1 else _pick_rows(x_refs[0], x_refs[1], n_prompt_tiles)
    x1 = x + gate_ref[...] * _dot(y_ref[...], w_ref[...])
    x1_ref[...] = x1
    if h_refs:
        h_refs[0][...] = (_rms(x1, g_ref[...]) * (1 + sc_ref[...]) + sh_ref[...]).astype(h_refs[0].dtype)


def _out_proj(y, w, x, mods, g, emit_h):
    tm = 512
    rows = pl.BlockSpec((tm, D_MODEL), lambda i: (i, 0))
    mod = lambda k: pl.BlockSpec((None, 1, D_MODEL), lambda i: (_cond_row(i, tm), 0, k))
    if isinstance(x, tuple):
        x_specs, n_p = _split_row_specs(tm, D_MODEL)
    else:
        x, x_specs, n_p = (x,), [rows], 0
    n_out = 2 if emit_h else 1
    return pl.pallas_call(
        functools.partial(_out_proj_kernel, n_x=len(x), n_prompt_tiles=n_p),
        grid=(T // tm,),
        in_specs=[rows, pl.BlockSpec((D_MODEL, D_MODEL), lambda i: (0, 0))] + x_specs + [
            mod(2), pl.BlockSpec((1, D_MODEL), lambda i: (0, 0)), mod(3), mod(4)],
        out_specs=[rows, rows][:n_out],
        out_shape=[jax.ShapeDtypeStruct((T, D_MODEL), F32), jax.ShapeDtypeStruct((T, D_MODEL), BF16)][:n_out],
        compiler_params=_cparams(("parallel",)),
        name="out_proj",
    )(y, w, *x, mods, g.reshape(1, D_MODEL), mods, mods)


def _swiglu_accumulate(x, wg_ref, wu_ref, wd_ref, o_ref, j):
    g = _dot(x, wg_ref[...].astype(BF16))
    u = _dot(x, wu_ref[...].astype(BF16))
    h = (g * jax.nn.sigmoid(g) * u).astype(BF16)

    @pl.when(j == 0)
    def _():
        o_ref[...] = _dot(h, wd_ref[...].astype(BF16))

    @pl.when(j > 0)
    def _():
        o_ref[...] += _dot(h, wd_ref[...].astype(BF16))


def _ffn_dense_kernel(x_ref, wg_ref, wu_ref, wd_ref, o_ref):
    _swiglu_accumulate(x_ref[...], wg_ref, wu_ref, wd_ref, o_ref, pl.program_id(1))


def _ffn_dense(h, wg, wu, wd):
    tm = TM_FFN
    return pl.pallas_call(
        _ffn_dense_kernel,
        grid=(T // tm, NF),
        in_specs=[pl.BlockSpec((tm, D_MODEL), lambda i, j: (i, 0)),
                  pl.BlockSpec((D_MODEL, TF), lambda i, j: (0, j)),
                  pl.BlockSpec((D_MODEL, TF), lambda i, j: (0, j)),
                  pl.BlockSpec((TF, D_MODEL), lambda i, j: (j, 0))],
        out_specs=pl.BlockSpec((tm, D_MODEL), lambda i, j: (i, 0)),
        out_shape=jax.ShapeDtypeStruct((T, D_MODEL), F32),
        compiler_params=_cparams(("parallel", "arbitrary")),
        name="ffn_dense",
    )(h, wg, wu, wd)


def _norm_router_kernel(x_ref, g_ref, sh_ref, sc_ref, whi_ref, wlo_ref, h_ref, meta_ref, wt_ref, cnt_ref, carry_ref):
    i = pl.program_id(0)

    @pl.when(i == 0)
    def _():
        carry_ref[...] = jnp.zeros_like(carry_ref)

    h = _rms(x_ref[...], g_ref[...]) * (1 + sc_ref[...]) + sh_ref[...]
    hi = h.astype(BF16)
    h_ref[...] = _pack_bf16_pairs(hi)
    lo = (h - hi.astype(F32)).astype(BF16)
    logits = _dot(hi, whi_ref[...]) + _dot(lo, whi_ref[...]) + _dot(hi, wlo_ref[...])
    lane = lax.broadcasted_iota(jnp.int32, logits.shape, 1)
    lg = jnp.where(lane < N_EXPERTS, logits, -jnp.inf)
    v1 = jnp.max(lg, axis=-1, keepdims=True)
    i1 = jnp.min(jnp.where(lg == v1, lane, LANES), axis=-1, keepdims=True)
    lg2 = jnp.where(lane == i1, -jnp.inf, lg)
    v2 = jnp.max(lg2, axis=-1, keepdims=True)
    i2 = jnp.min(jnp.where(lg2 == v2, lane, LANES), axis=-1, keepdims=True)
    e2 = jnp.exp(v2 - v1)
    tot = 1.0 + e2
    wt_ref[...] = jnp.where(lane == 0, 1.0 / tot, jnp.where(lane == 1, e2 / tot, 0.0))
    oh1 = (lane == i1).astype(F32)
    oh2 = (lane == i2).astype(F32)
    r = lax.broadcasted_iota(jnp.int32, (TR, TR), 0)
    c = lax.broadcasted_iota(jnp.int32, (TR, TR), 1)
    earlier = (r > c).astype(BF16)
    base = carry_ref[0:1, :]
    c1 = jnp.sum(oh1, axis=0, keepdims=True)
    c2 = jnp.sum(oh2, axis=0, keepdims=True)
    r1 = jnp.sum(oh1 * (base + _dot(earlier, oh1.astype(BF16))), axis=-1, keepdims=True)
    r2 = jnp.sum(oh2 * (base + c1 + _dot(earlier, oh2.astype(BF16))), axis=-1, keepdims=True)
    total = jnp.broadcast_to(base + c1 + c2, carry_ref.shape)
    carry_ref[...] = total
    cnt_ref[...] = total
    meta_ref[...] = jnp.where(lane == 0, i1, jnp.where(lane == 1, i2, jnp.where(
        lane == 2, r1.astype(jnp.int32), jnp.where(lane == 3, r2.astype(jnp.int32), 0))))


def _norm_router(x, g, mods, w_hi, w_lo):
    rows = lambda c: pl.BlockSpec((TR, c), lambda i: (i, 0))
    full = lambda shape: pl.BlockSpec(shape, lambda i: (0, 0))
    mod = lambda k: pl.BlockSpec((None, 1, D_MODEL), lambda i: (_cond_row(i, TR), 0, k))
    return pl.pallas_call(
        _norm_router_kernel,
        grid=(T // TR,),
        in_specs=[rows(D_MODEL), full((1, D_MODEL)), mod(3), mod(4), full((D_MODEL, LANES)), full((D_MODEL, LANES))],
        out_specs=[rows(D_MODEL // 2), rows(LANES), rows(LANES), full((8, LANES))],
        out_shape=[jax.ShapeDtypeStruct((T, D_MODEL // 2), jnp.uint32),
                   jax.ShapeDtypeStruct((T, LANES), jnp.int32),
                   jax.ShapeDtypeStruct((T, LANES), F32),
                   jax.ShapeDtypeStruct((8, LANES), F32)],
        scratch_shapes=[pltpu.VMEM((8, LANES), F32)],
        compiler_params=_cparams(("arbitrary",)),
        name="norm_router",
    )(x, g.reshape(1, D_MODEL), mods, mods, w_hi, w_lo)


def _dispatch(meta, counts):
    experts = jnp.arange(N_EXPERTS, dtype=jnp.int32)
    cnt = counts[0, :N_EXPERTS].astype(jnp.int32)
    padded = ((cnt + TM_FFN - 1) // TM_FFN) * TM_FFN
    g_end = jnp.sum(jnp.where(experts[None, :] <= experts[:, None], padded[None, :], 0), axis=1)
    g_start = g_end - padded
    e = meta[:, :TOP_K]
    start_of = jnp.sum(jnp.where(e[:, :, None] == experts[None, None, :], g_start[None, None, :], 0), axis=-1)
    slot = start_of + meta[:, TOP_K:2 * TOP_K]
    end_tiles = g_end // TM_FFN
    n_used = end_tiles[N_EXPERTS - 1]
    tiles = jnp.arange(MOE_TILES, dtype=jnp.int32)
    owner = lambda t: jnp.minimum(jnp.sum((end_tiles[None, :] <= t[:, None]).astype(jnp.int32), axis=1), N_EXPERTS - 1)
    tile_e = owner(jnp.minimum(tiles, n_used - 1))
    return slot, g_end, padded, tile_e, n_used.reshape(1)


def _row_copy(src, src_row, dst, dst_row, sem):
    return pltpu.make_async_copy(src.at[pl.ds(src_row, 1), :], dst.at[pl.ds(dst_row, 1), :], sem)


def _moe_scatter_kernel(ge_ref, pd_ref, slot_hbm, h_ref, xs_hbm, slot_smem, zero_ref, sem_ids, sem_rows, sem_zero):
    i = pl.program_id(0)

    def clear_copies(first_row):
        base = pl.multiple_of(first_row, ZR)
        return [pltpu.make_async_copy(zero_ref, xs_hbm.at[pl.ds(base + k * ZR, ZR), :], sem_zero)
                for k in range(TM_FFN // ZR)]

    def for_unused_tiles(fn):
        def body(t, carry):
            for cp in clear_copies(t * TM_FFN):
                fn(cp)
            return carry
        lax.fori_loop(ge_ref[N_EXPERTS - 1] // TM_FFN, MOE_TILES, body, 0)

    @pl.when(i == 0)
    def _():
        zero_ref[...] = jnp.zeros_like(zero_ref)
        for e in range(N_EXPERTS):
            @pl.when(pd_ref[e] > 0)
            def _():
                for cp in clear_copies(ge_ref[e] - TM_FFN):
                    cp.start()
        for_unused_tiles(lambda cp: cp.start())
        for e in range(N_EXPERTS):
            @pl.when(pd_ref[e] > 0)
            def _():
                for cp in clear_copies(ge_ref[e] - TM_FFN):
                    cp.wait()
        for_unused_tiles(lambda cp: cp.wait())

    ids_copy = pltpu.make_async_copy(slot_hbm.at[i], slot_smem, sem_ids)
    ids_copy.start()
    ids_copy.wait()

    def start(r, carry):
        _row_copy(h_ref, r, xs_hbm, slot_smem[TOP_K * r], sem_rows).start(priority=0)
        _row_copy(h_ref, r, xs_hbm, slot_smem[TOP_K * r + 1], sem_rows).start(priority=1)
        return carry

    lax.fori_loop(0, TR, start, 0)

    def wait(r, carry):
        _row_copy(h_ref, r, xs_hbm, 0, sem_rows).wait()
        _row_copy(h_ref, r, xs_hbm, 0, sem_rows).wait()
        return carry

    lax.fori_loop(0, TR, wait, 0)


def _moe_scatter(h, slot, g_end, padded):
    grid_spec = pltpu.PrefetchScalarGridSpec(
        num_scalar_prefetch=2,
        grid=(T // TR,),
        in_specs=[pl.BlockSpec(memory_space=pl.ANY),
                  pl.BlockSpec((TR, D_MODEL // 2), lambda i, ge, pd: (i, 0))],
        out_specs=pl.BlockSpec(memory_space=pl.ANY),
        scratch_shapes=[pltpu.SMEM((TOP_K * TR,), jnp.int32),
                        pltpu.VMEM((ZR, D_MODEL // 2), jnp.uint32),
                        pltpu.SemaphoreType.DMA,
                        pltpu.SemaphoreType.DMA,
                        pltpu.SemaphoreType.DMA],
    )
    return pl.pallas_call(
        _moe_scatter_kernel,
        grid_spec=grid_spec,
        out_shape=jax.ShapeDtypeStruct((MOE_TILES * TM_FFN, D_MODEL // 2), jnp.uint32),
        compiler_params=_cparams(("arbitrary",)),
        name="moe_scatter",
    )(g_end, padded, slot.reshape(T // TR, TOP_K * TR), h)


def _ffn_moe_kernel(te_ref, nu_ref, xs_hbm, wg_ref, wu_ref, wd_ref, o_ref, xg_ref, xb_ref, sem):
    i = pl.program_id(0)
    j = pl.program_id(1)
    n_used = nu_ref[0]
    used = i < n_used

    def tile_copy(t):
        return pltpu.make_async_copy(xs_hbm.at[pl.ds(pl.multiple_of(t * TM_FFN, TM_FFN), TM_FFN), :], xg_ref, sem)

    @pl.when(used & (j == 0))
    def _():
        @pl.when(i == 0)
        def _():
            tile_copy(0).start()

        tile_copy(i).wait()
        _unpack_bf16_pairs(xg_ref[...], xb_ref)

        @pl.when(i + 1 < n_used)
        def _():
            tile_copy(i + 1).start()

    @pl.when(used)
    def _():
        _swiglu_accumulate(xb_ref[...], wg_ref, wu_ref, wd_ref, o_ref, j)

    @pl.when(jnp.logical_not(used) & (j == 0))
    def _():
        o_ref[...] = jnp.zeros_like(o_ref)


def _ffn_moe(xs, tile_e, n_used, wg, wu, wd):
    nf = D_FF // TF_MOE

    def hid(i, j, te, nu):
        return jnp.where(i < nu[0], j, nf - 1)

    grid_spec = pltpu.PrefetchScalarGridSpec(
        num_scalar_prefetch=2,
        grid=(MOE_TILES, nf),
        in_specs=[pl.BlockSpec(memory_space=pl.ANY),
                  pl.BlockSpec((None, None, D_MODEL, TF_MOE), lambda i, j, te, nu: (0, te[i], 0, hid(i, j, te, nu))),
                  pl.BlockSpec((None, None, D_MODEL, TF_MOE), lambda i, j, te, nu: (0, te[i], 0, hid(i, j, te, nu))),
                  pl.BlockSpec((None, None, TF_MOE, D_MODEL), lambda i, j, te, nu: (0, te[i], hid(i, j, te, nu), 0))],
        out_specs=pl.BlockSpec((TM_FFN, D_MODEL), lambda i, j, te, nu: (i, 0)),
        scratch_shapes=[pltpu.VMEM((TM_FFN, D_MODEL // 2), jnp.uint32),
                        pltpu.VMEM((TM_FFN, D_MODEL), BF16),
                        pltpu.SemaphoreType.DMA],
    )
    return pl.pallas_call(
        _ffn_moe_kernel,
        grid_spec=grid_spec,
        out_shape=jax.ShapeDtypeStruct((MOE_TILES * TM_FFN, D_MODEL), F32),
        compiler_params=_cparams(("arbitrary", "arbitrary")),
        name="ffn_moe",
    )(tile_e, n_used, xs, wg, wu, wd)


def _combine_kernel(slot_hbm, ys_hbm, wt_ref, x_ref, gate_ref, g_ref, yp_ref, ysm_ref,
                    slot_smem, y0_ref, y1_ref, sem_ids, sem_rows, *, n_prompt_tiles):
    i = pl.program_id(0)

    def fetch(t):
        b = t % 2
        ids_copy = pltpu.make_async_copy(slot_hbm.at[t], slot_smem.at[b], sem_ids)
        ids_copy.start()
        ids_copy.wait()

        def start(r, carry):
            _row_copy(ys_hbm, slot_smem[b, TOP_K * r], y0_ref.at[b], r, sem_rows.at[b]).start(priority=0)
            _row_copy(ys_hbm, slot_smem[b, TOP_K * r + 1], y1_ref.at[b], r, sem_rows.at[b]).start(priority=1)
            return carry

        lax.fori_loop(0, TC, start, 0)

    @pl.when(i == 0)
    def _():
        fetch(0)

    @pl.when(i + 1 < pl.num_programs(0))
    def _():
        fetch(i + 1)

    b = i % 2

    def wait(r, carry):
        _row_copy(ys_hbm, 0, y0_ref.at[b], r, sem_rows.at[b]).wait()
        _row_copy(ys_hbm, 0, y1_ref.at[b], r, sem_rows.at[b]).wait()
        return carry

    lax.fori_loop(0, TC, wait, 0)
    wt = wt_ref[...]
    y = wt[:, 0:1] * y0_ref[b] + wt[:, 1:2] * y1_ref[b]
    out = _rms(x_ref[...] + gate_ref[...] * y, g_ref[...])

    @pl.when(i < n_prompt_tiles)
    def _():
        yp_ref[...] = out

    @pl.when(i >= n_prompt_tiles)
    def _():
        ysm_ref[...] = out


def _combine(slot, ys, wt, x, mods, k_gate, g):
    n_p = TP // TC
    return pl.pallas_call(
        functools.partial(_combine_kernel, n_prompt_tiles=n_p),
        grid=(T // TC,),
        in_specs=[pl.BlockSpec(memory_space=pl.ANY),
                  pl.BlockSpec(memory_space=pl.ANY),
                  pl.BlockSpec((TC, LANES), lambda i: (i, 0)),
                  pl.BlockSpec((TC, D_MODEL), lambda i: (i, 0)),
                  pl.BlockSpec((None, 1, D_MODEL), lambda i: (_cond_row(i, TC), 0, k_gate)),
                  pl.BlockSpec((1, D_MODEL), lambda i: (0, 0))],
        out_specs=[pl.BlockSpec((TC, D_MODEL), lambda i: (jnp.minimum(i, n_p - 1), 0)),
                   pl.BlockSpec((TC, D_MODEL), lambda i: (jnp.maximum(i - n_p, 0), 0))],
        out_shape=[jax.ShapeDtypeStruct((TP, D_MODEL), F32), jax.ShapeDtypeStruct((TS, D_MODEL), F32)],
        scratch_shapes=[pltpu.SMEM((2, TOP_K * TC), jnp.int32),
                        pltpu.VMEM((2, TC, D_MODEL), F32),
                        pltpu.VMEM((2, TC, D_MODEL), F32),
                        pltpu.SemaphoreType.DMA,
                        pltpu.SemaphoreType.DMA((2,))],
        compiler_params=_cparams(("arbitrary",)),
        name="moe_combine",
    )(slot.reshape(T // TC, TOP_K * TC), ys, wt, x, mods, g.reshape(1, D_MODEL))


def kernel(x_prompt, x_sample, cache_attn_k, cache_attn_v, cache_mla_ckv, cache_mla_krope, c, c_ctx, ln1_g, ln2_g, w_ada, b_ada, w_in, attn_sink, mla_q_norm_g, w_uq, mla_kv_norm_g, w_ukv, pool_w, pool_scale, w_branch_a, w_branch_b, w_branch_c, w_out, ffn_w_gate, ffn_w_up, ffn_w_down, router_w, moe_w_gate, moe_w_up, moe_w_down, final_g):
    x = (x_prompt.reshape(TP, D_MODEL), x_sample.reshape(TS, D_MODEL))
    w_in_t = jnp.swapaxes(w_in, 1, 2)
    cond =jnp.concatenate([c_ctx[None, :], c, jnp.zeros((N_COND - 1 - DEC_BATCH, D_MODEL), F32)], axis=0)
    rows = DEC_SEQ // GRID_W
    cos_a, sin_a = _rope_tables(rows, HD_A)
    cos_b, sin_b = _rope_tables(rows, QK_ROPE)
    st_k, st_v, st_ckv, st_kr = [], [], [], []
    assert DEPTH == 2, "layer 0 is the dense-FFN layer, layer 1 the expert layer that ends the trunk"
    all_mods = [_adaln(cond, w_ada, b_ada, l).reshape(N_COND, 1, 6 * D_MODEL) for l in range(DEPTH)]
    cos_t = jnp.concatenate([jnp.ones((TM, LANES), F32), cos_b], axis=0)
    sin_t = jnp.concatenate([jnp.zeros((TM, LANES), F32), sin_b], axis=0)
    h1 = _norm_mod(x[0], x[1], ln1_g[0], all_mods[0], 0, 1)
    for l in range(DEPTH):
        mods = all_mods[l]
        qa = _matmul_wt(h1, w_in_t, l, 0, A_W, A_W, F32, "proj_q")
        kv = _matmul_wt(h1, w_in_t, l, OFF_K, 2 * KV_W, 2 * KV_W, F32, "proj_kv")
        zb = _matmul_wt(h1, w_in_t, l, OFF_CQ, OFF_POOL - OFF_CQ, OFF_POOL - OFF_CQ, F32, "proj_mla")
        u = _matmul_wt(h1, w_in_t, l, OFF_POOL, C_W, C_W, F32, "proj_pool")
        gates = _matmul_wt(h1, w_in_t, l, OFF_GATE, 3 * D_MODEL, 1024, BF16, "proj_gates")
        sink = attn_sink[l]
        ck = cache_attn_k[:, l].reshape(DEC_BATCH, PAST_LEN, KV_W)
        cv = cache_attn_v[:, l].reshape(DEC_BATCH, PAST_LEN, KV_W)
        oa = (_attn_a_prompt(qa, kv, sink), _attn_a_sample(qa, kv, ck, cv, sink, cos_a, sin_a))
        wq_cat, w_aug = _mla_weights(w_uq[l], w_ukv[l])
        q, ckv, kr, xa = _mla_prep(zb, mla_q_norm_g[l], mla_kv_norm_g[l], wq_cat, cos_t, sin_t)
        cache_aug = jnp.concatenate([cache_mla_ckv[:, l], cache_mla_krope[:, l]], axis=-1).astype(BF16)
        xa_s = jnp.concatenate([xa[TP:].reshape(DEC_BATCH, DEC_SEQ, MLA_AUG), cache_aug], axis=1)
        kv_p = _matmul(xa[:TP], w_aug, BF16, TM, B_W, "mla_kv_prompt")
        kv_s = _matmul(xa_s.reshape(DEC_BATCH * MLA_KEYS, MLA_AUG), w_aug, BF16, TM, B_W, "mla_kv_sample")
        ob = (_mla_attend(q, kv_p, SEQ, 0, BATCH, 1, SEQ, "mla_prompt"),
              _mla_attend(q, kv_s, TQ_MLA, TP // TQ_MLA, DEC_BATCH, DEC_SEQ // TQ_MLA, MLA_KEYS, "mla_sample"))
        pw = pool_w[l].astype(BF16)
        oc = (_pool(u, pw, pool_scale[l], SEQ, BATCH, 0, "pool_prompt"),
              _pool(u, pw, pool_scale[l], DEC_SEQ, DEC_BATCH, TP // DEC_SEQ, "pool_sample"))
        y = _merge((oa[0], ob[0], oc[0]), (oa[1], ob[1], oc[1]), gates, w_branch_a[l].astype(BF16),
                   w_branch_b[l].astype(BF16), w_branch_c[l].astype(BF16))
        st_k.append(kv[:TP, :KV_W].reshape(BATCH, SEQ, N_KV_A, HD_A))
        st_v.append(kv[:TP, KV_W:].reshape(BATCH, SEQ, N_KV_A, HD_A))
        st_ckv.append(ckv[:TP].reshape(BATCH, SEQ, KV_LORA))
        st_kr.append(kr[:TP].reshape(BATCH, SEQ, QK_ROPE))
        if l == 0:
            x, h2 = _out_proj(y, w_out[l].astype(BF16), x, mods, ln2_g[l], True)
            f = _ffn_dense(h2, ffn_w_gate[0].astype(BF16), ffn_w_up[0].astype(BF16), ffn_w_down[0].astype(BF16))
            x, h1 = _resid_norm(x, f, mods, 5, ln1_g[1], all_mods[1], 0, 1)
        else:
            (x,) = _out_proj(y, w_out[l].astype(BF16), x, mods, ln2_g[l], False)
            rw = jnp.pad(router_w[0], ((0, 0), (0, LANES - N_EXPERTS)))
            rw_hi = rw.astype(BF16)
            rw_lo = (rw - rw_hi.astype(F32)).astype(BF16)
            h2, meta, wt, counts = _norm_router(x, ln2_g[l], mods, rw_hi, rw_lo)
            slot, g_end, padded, tile_e, n_used = _dispatch(meta, counts)
            xs = _moe_scatter(h2, slot, g_end, padded)
            ys = _ffn_moe(xs, tile_e, n_used, moe_w_gate, moe_w_up, moe_w_down)
            y_p, y_s = _combine(slot, ys, wt, x, mods, 5, final_g)
    return (y_p.reshape(BATCH, SEQ, D_MODEL), y_s.reshape(DEC_BATCH, DEC_SEQ, D_MODEL),
            jnp.stack(st_k, axis=1), jnp.stack(st_v, axis=1), jnp.stack(st_ckv, axis=1), jnp.stack(st_kr, axis=1))
```

```python
import functools
import math

import jax
import jax.numpy as jnp
from jax import lax
from jax.experimental import pallas as pl
from jax.experimental.pallas import tpu as pltpu

BF16 = jnp.bfloat16
F32 = jnp.float32

D_MODEL = 2048
BATCH = 16
SEQ = 256
DEPTH = 2
DEC_BATCH = 8
DEC_SEQ = 2048
PAST_LEN = 512
GRID_W = 64
BLK = 128
EPS = 1e-6
ROPE_BASE = 10000.0
NEG_INF = -1e30
N_HEADS_A = 16
N_KV_A = 2
HD_A = 64
WINDOW = 128
A_W = N_HEADS_A * HD_A
KV_W = N_KV_A * HD_A
N_HEADS_B = 8
Q_LORA = 512
KV_LORA = 256
QK_NOPE = 128
QK_ROPE = 64
V_HD = 128
B_W = N_HEADS_B * V_HD
POOL_WINDOWS = (2, 4, 8, 16)
POOL_GROUPS = 4
POOL_GW = 256
C_W = POOL_GROUPS * POOL_GW
OFF_K = A_W
OFF_V = OFF_K + KV_W
OFF_CQ = OFF_V + KV_W
OFF_CKV = OFF_CQ + Q_LORA
OFF_KR = OFF_CKV + KV_LORA
OFF_POOL = OFF_KR + QK_ROPE
OFF_GATE = OFF_POOL + C_W
IN_COLS = OFF_GATE + 3 * D_MODEL
D_FF = 5632
N_EXPERTS = 8
TOP_K = 2

TP = BATCH * SEQ
TS = DEC_BATCH * DEC_SEQ
T = TP + TS
N_COND = 16
MLA_KEYS = DEC_SEQ + PAST_LEN
LANES = 128
POOL_HALO = 8
LOG2E = math.log2(math.e)
MLA_HEAD_K = 256
MLA_AUG = KV_LORA + QK_ROPE

VMEM_LIMIT = 56 * 1024 * 1024

TM = 1024
TM_FFN = 1024
TF = 512
NF = D_FF // TF
TF_MOE = 256
MOE_TILES = (TOP_K * T) // TM_FFN + N_EXPERTS
TR = 512
ZR = 256
TC = 256
TQ_MLA = 512
TQ_A = 256


def _cparams(sem):
    return pltpu.CompilerParams(dimension_semantics=sem, vmem_limit_bytes=VMEM_LIMIT)


def _cond_row(i, tm):
    n_p = TP // tm
    per_b = DEC_SEQ // tm
    return jnp.where(i < n_p, 0, (i - n_p) // per_b + 1)


def _dot(a, b):
    return jnp.dot(a, b, preferred_element_type=F32)


def _dot_nt(a, b):
    return lax.dot_general(a, b, (((1,), (1,)), ((), ())), preferred_element_type=F32)


def _mm_kernel(x_ref, w_ref, o_ref):
    o_ref[...] = _dot(x_ref[...].astype(BF16), w_ref[...].astype(BF16)).astype(o_ref.dtype)


def _matmul(x, w, out_dtype, tm, tn, name):
    m, k = x.shape
    n = w.shape[1]
    return pl.pallas_call(
        _mm_kernel,
        grid=(m // tm, n // tn),
        in_specs=[pl.BlockSpec((tm, k), lambda i, j: (i, 0)),
                  pl.BlockSpec((k, tn), lambda i, j: (0, j))],
        out_specs=pl.BlockSpec((tm, tn), lambda i, j: (i, j)),
        out_shape=jax.ShapeDtypeStruct((m, n), out_dtype),
        compiler_params=_cparams(("parallel", "arbitrary")),
        name=name,
    )(x, w)


def _mm_wt_kernel(x_ref, wt_hbm, o_ref, stage_ref, wb_ref, sem, *, layer, row0, tn):
    j = pl.program_id(0)

    def tile_copy(t):
        first = pl.multiple_of(row0 + t * tn, 8)
        return pltpu.make_async_copy(wt_hbm.at[layer, pl.ds(first, tn), :], stage_ref, sem)

    @pl.when(pl.program_id(1) == 0)
    def _():
        @pl.when(j == 0)
        def _():
            tile_copy(0).start()

        tile_copy(j).wait()
        wb_ref[...] = stage_ref[...].astype(BF16)

        @pl.when(j + 1 < pl.num_programs(0))
        def _():
            tile_copy(j + 1).start()

    o_ref[...] = _dot_nt(x_ref[...], wb_ref[...]).astype(o_ref.dtype)


def _matmul_wt(x, wt, layer, row0, n, tn, out_dtype, name):
    m, k = x.shape
    tm = TM
    assert n % tn == 0 and row0 % 8 == 0 and tn % 16 == 0
    return pl.pallas_call(
        functools.partial(_mm_wt_kernel, layer=layer, row0=row0, tn=tn),
        grid=(n // tn, m // tm),
        in_specs=[pl.BlockSpec((tm, k), lambda j, i: (i, 0)), pl.BlockSpec(memory_space=pl.ANY)],
        out_specs=pl.BlockSpec((tm, tn), lambda j, i: (i, j)),
        out_shape=jax.ShapeDtypeStruct((m, n), out_dtype),
        scratch_shapes=[pltpu.VMEM((tn, k), F32), pltpu.VMEM((tn, k), BF16), pltpu.SemaphoreType.DMA],
        compiler_params=_cparams(("arbitrary", "arbitrary")),
        name=name,
    )(x, wt)


def _ada_kernel(c_ref, w_ref, b_ref, o_ref):
    c = c_ref[...]
    a = (c * jax.nn.sigmoid(c)).astype(BF16)
    o_ref[...] = _dot(a, w_ref[...].astype(BF16)) + b_ref[...]


def _adaln(cond, w, b, layer):
    n = w.shape[2]
    tn = 1024
    return pl.pallas_call(
        _ada_kernel,
        grid=(n // tn,),
        in_specs=[pl.BlockSpec((N_COND, D_MODEL), lambda j: (0, 0)),
                  pl.BlockSpec((None, D_MODEL, tn), lambda j: (layer, 0, j)),
                  pl.BlockSpec((None, 1, tn), lambda j: (layer, 0, j))],
        out_specs=pl.BlockSpec((N_COND, tn), lambda j: (0, j)),
        out_shape=jax.ShapeDtypeStruct((N_COND, n), F32),
        compiler_params=_cparams(("arbitrary",)),
        name="adaln",
    )(cond, w, b.reshape(DEPTH, 1, n))


def _split_row_specs(tm, width):
    n_p = TP // tm
    return [pl.BlockSpec((tm, width), lambda i, *_: (jnp.minimum(i, n_p - 1), 0)),
            pl.BlockSpec((tm, width), lambda i, *_: (jnp.maximum(i - n_p, 0), 0))], n_p


def _pick_rows(p_ref, s_ref, n_prompt_tiles):
    return jnp.where(pl.program_id(0) < n_prompt_tiles, p_ref[...], s_ref[...])


def _norm_mod_kernel(xp_ref, xs_ref, g_ref, sh_ref, sc_ref, o_ref, *, n_prompt_tiles):
    x = _pick_rows(xp_ref, xs_ref, n_prompt_tiles)
    o_ref[...] = (_rms(x, g_ref[...]) * (1 + sc_ref[...]) + sh_ref[...]).astype(o_ref.dtype)


def _norm_mod(xp, xs, g, mods, k_shift, k_scale):
    tm = 512
    x_specs, n_p = _split_row_specs(tm, D_MODEL)
    return pl.pallas_call(
        functools.partial(_norm_mod_kernel, n_prompt_tiles=n_p),
        grid=(T // tm,),
        in_specs=x_specs + [pl.BlockSpec((1, D_MODEL), lambda i: (0, 0)),
                            pl.BlockSpec((None, 1, D_MODEL), lambda i: (_cond_row(i, tm), 0, k_shift)),
                            pl.BlockSpec((None, 1, D_MODEL), lambda i: (_cond_row(i, tm), 0, k_scale))],
        out_specs=pl.BlockSpec((tm, D_MODEL), lambda i: (i, 0)),
        out_shape=jax.ShapeDtypeStruct((T, D_MODEL), BF16),
        compiler_params=_cparams(("parallel",)),
        name="norm_mod1",
    )(xp, xs, g.reshape(1, D_MODEL), mods, mods)


def _rms(x, g):
    return x * lax.rsqrt(jnp.mean(x * x, axis=-1, keepdims=True) + EPS) * g


def _resid_norm_kernel(x_ref, f_ref, gate_ref, g_ref, sh_ref, sc_ref, x2_ref, h_ref):
    x2 = x_ref[...] + gate_ref[...] * f_ref[...]
    x2_ref[...] = x2
    h_ref[...] = (_rms(x2, g_ref[...]) * (1 + sc_ref[...]) + sh_ref[...]).astype(h_ref.dtype)


def _resid_norm(x, f, mods, k_gate, g, mods_next, k_shift, k_scale):
    tm = 512
    rows = pl.BlockSpec((tm, D_MODEL), lambda i: (i, 0))
    mod = lambda k: pl.BlockSpec((None, 1, D_MODEL), lambda i: (_cond_row(i, tm), 0, k))
    return pl.pallas_call(
        _resid_norm_kernel,
        grid=(T // tm,),
        in_specs=[rows, rows, mod(k_gate), pl.BlockSpec((1, D_MODEL), lambda i: (0, 0)), mod(k_shift), mod(k_scale)],
        out_specs=[rows, rows],
        out_shape=[jax.ShapeDtypeStruct((T, D_MODEL), F32), jax.ShapeDtypeStruct((T, D_MODEL), BF16)],
        compiler_params=_cparams(("parallel",)),
        name="resid_norm",
    )(x, f, mods, g.reshape(1, D_MODEL), mods_next, mods_next)


def _rope_tables(rows, dim):
    quarter = dim // 4
    inv = ROPE_BASE ** (-jnp.arange(quarter, dtype=F32) / quarter)
    r = jnp.repeat(jnp.arange(rows, dtype=F32), GRID_W)
    col = jnp.tile(jnp.arange(GRID_W, dtype=F32), rows)
    ar, ac = r[:, None] * inv, col[:, None] * inv
    cos = jnp.concatenate([jnp.cos(ar), jnp.cos(ar), jnp.cos(ac), jnp.cos(ac)], axis=-1)
    sin = jnp.concatenate([-jnp.sin(ar), jnp.sin(ar), -jnp.sin(ac), jnp.sin(ac)], axis=-1)
    reps = LANES // dim
    return jnp.tile(cos, (1, reps)), jnp.tile(sin, (1, reps))


def _rope_lanes(x, cos, sin, quarter):
    lane = lax.broadcasted_iota(jnp.int32, (x.shape[0], LANES), 1)
    first = (lane % (2 * quarter)) < quarter
    outs = []
    for c in range(x.shape[1] // LANES):
        xc = x[:, c * LANES:(c + 1) * LANES]
        partner = jnp.where(first, pltpu.roll(xc, LANES - quarter, 1), pltpu.roll(xc, quarter, 1))
        outs.append(xc * cos + partner * sin)
    return outs[0] if len(outs) == 1 else jnp.concatenate(outs, axis=1)


def _attn_a_kernel(*refs, local, tq, n_blocks):
    assert N_KV_A * HD_A == LANES
    if local:
        (sink_ref, q_ref, kc_ref, vc_ref, kvm_ref, kv0_ref, kvp_ref,
         cq_ref, sq_ref, ckm_ref, skm_ref, ck0_ref, sk0_ref, ckp_ref, skp_ref, o_ref) = refs
    else:
        sink_ref, q_ref, kc_ref, vc_ref, o_ref = refs
    quarter = HD_A // 4
    q = q_ref[...]
    k_all = kc_ref[...]
    v_all = vc_ref[...]
    n_loc = 0
    if local:
        j = pl.program_id(1)
        n_loc = tq + 2 * BLK
        q = _rope_lanes(q, cq_ref[...], sq_ref[...], quarter)
        k_all = jnp.concatenate([
            _rope_lanes(kvm_ref[:, :KV_W], ckm_ref[...], skm_ref[...], quarter),
            _rope_lanes(kv0_ref[:, :KV_W], ck0_ref[...], sk0_ref[...], quarter),
            _rope_lanes(kvp_ref[:, :KV_W], ckp_ref[...], skp_ref[...], quarter), k_all], axis=0)
        v_all = jnp.concatenate([kvm_ref[:, KV_W:], kv0_ref[:, KV_W:], kvp_ref[:, KV_W:], v_all], axis=0)
        qi = lax.broadcasted_iota(jnp.int32, (tq, n_loc), 0)
        ki = lax.broadcasted_iota(jnp.int32, (tq, n_loc), 1)
        k_lo = jnp.where(j == 0, BLK, 0)
        k_hi = jnp.where(j == n_blocks - 1, n_loc - BLK, n_loc)
        valid = (ki >= qi) & (ki <= qi + 2 * WINDOW) & (ki >= k_lo) & (ki < k_hi)
        bias = jnp.where(valid, 0.0, NEG_INF)
    q = (q * (HD_A ** -0.5 * LOG2E)).astype(BF16)
    nk = k_all.shape[0]
    low_k = lax.broadcasted_iota(jnp.int32, (nk, LANES), 1) < HD_A
    low_q = lax.broadcasted_iota(jnp.int32, (tq, LANES), 1) < HD_A
    rep = N_HEADS_A // N_KV_A

    def blockdiag(x, g):
        swapped = pltpu.roll(x, HD_A, 1)
        lo_src, hi_src = (x, swapped) if g == 0 else (swapped, x)
        return jnp.concatenate([jnp.where(low_k, lo_src, 0.0), jnp.where(low_k, 0.0, hi_src)], axis=0).astype(BF16)

    for g in range(N_KV_A):
        k2 = blockdiag(k_all, g)
        v2 = blockdiag(v_all, g)
        for p in range(rep // 2):
            h0 = g * rep + 2 * p
            s = _dot_nt(q[:, h0 * HD_A:h0 * HD_A + LANES], k2)
            parts, inv = [], []
            for t in range(2):
                sh = s[:, t * nk:(t + 1) * nk]
                sk = sink_ref[h0 + t] * LOG2E
                if local:
                    s_loc = sh[:, :n_loc] + bias
                    s_ctx = sh[:, n_loc:]
                    m = jnp.maximum(jnp.maximum(jnp.max(s_loc, axis=-1, keepdims=True),
                                                jnp.max(s_ctx, axis=-1, keepdims=True)), sk)
                    p_loc = jnp.exp2(s_loc - m)
                    p_ctx = jnp.exp2(s_ctx - m)
                    denom = (jnp.sum(p_loc, axis=-1, keepdims=True) + jnp.sum(p_ctx, axis=-1, keepdims=True)
                             + jnp.exp2(sk - m))
                    parts += [p_loc, p_ctx]
                else:
                    m = jnp.maximum(jnp.max(sh, axis=-1, keepdims=True), sk)
                    p_all = jnp.exp2(sh - m)
                    denom = jnp.sum(p_all, axis=-1, keepdims=True) + jnp.exp2(sk - m)
                    parts.append(p_all)
                inv.append(1.0 / denom)
            o2 = _dot(jnp.concatenate(parts, axis=1).astype(BF16), v2) * jnp.where(low_q, inv[0], inv[1])
            o_ref[:, h0 * HD_A:h0 * HD_A + LANES] = o2.astype(o_ref.dtype)


def _attn_a_prompt(qa, kv, sink):
    kern = functools.partial(_attn_a_kernel, local=False, tq=SEQ, n_blocks=1)
    return pl.pallas_call(
        kern,
        grid=(BATCH,),
        in_specs=[pl.BlockSpec(memory_space=pltpu.SMEM),
                  pl.BlockSpec((SEQ, A_W), lambda b: (b, 0)),
                  pl.BlockSpec((SEQ, KV_W), lambda b: (b, 0)),
                  pl.BlockSpec((SEQ, KV_W), lambda b: (b, 1))],
        out_specs=pl.BlockSpec((SEQ, A_W), lambda b: (b, 0)),
        out_shape=jax.ShapeDtypeStruct((TP, A_W), BF16),
        compiler_params=_cparams(("parallel",)),
        name="attn_a_prompt",
    )(sink, qa, kv, kv)


def _attn_a_sample(qa, kv, ck, cv, sink, cos, sin):
    tq = TQ_A
    qb = tq // BLK
    nb = DEC_SEQ // BLK
    nq = DEC_SEQ // tq
    base_blk = TP // BLK
    base_q = TP // tq

    def prev_blk(j):
        return jnp.maximum(qb * j - 1, 0)

    def next_blk(j):
        return jnp.minimum(qb * j + qb, nb - 1)

    edge_kv = lambda f: pl.BlockSpec((BLK, 2 * KV_W), lambda b, j: (base_blk + b * nb + f(j), 0))
    edge_tab = lambda f: pl.BlockSpec((BLK, LANES), lambda b, j: (f(j), 0))
    own_tab = pl.BlockSpec((tq, LANES), lambda b, j: (j, 0))
    kern = functools.partial(_attn_a_kernel, local=True, tq=tq, n_blocks=nq)
    return pl.pallas_call(
        kern,
        grid=(DEC_BATCH, nq),
        in_specs=[pl.BlockSpec(memory_space=pltpu.SMEM),
                  pl.BlockSpec((tq, A_W), lambda b, j: (base_q + b * nq + j, 0)),
                  pl.BlockSpec((None, PAST_LEN, KV_W), lambda b, j: (b, 0, 0)),
                  pl.BlockSpec((None, PAST_LEN, KV_W), lambda b, j: (b, 0, 0)),
                  edge_kv(prev_blk),
                  pl.BlockSpec((tq, 2 * KV_W), lambda b, j: (base_q + b * nq + j, 0)),
                  edge_kv(next_blk),
                  own_tab, own_tab, edge_tab(prev_blk), edge_tab(prev_blk),
                  own_tab, own_tab, edge_tab(next_blk), edge_tab(next_blk)],
        out_specs=pl.BlockSpec((tq, A_W), lambda b, j: (b * nq + j, 0)),
        out_shape=jax.ShapeDtypeStruct((TS, A_W), BF16),
        compiler_params=_cparams(("parallel", "arbitrary")),
        name="attn_a_sample",
    )(sink, qa, ck, cv, kv, kv, kv, cos, sin, cos, sin, cos, sin, cos, sin)


def _mla_prep_kernel(z_ref, gq_ref, gkv_ref, wq_ref, cos_ref, sin_ref, q_ref, ckv_ref, kr_ref, xa_ref):
    quarter = QK_ROPE // 4
    c = (QK_NOPE + QK_ROPE) ** -0.5 * LOG2E
    cos = cos_ref[...]
    sin = sin_ref[...]
    cq = z_ref[:, :Q_LORA]
    q = _dot(_rms(cq, gq_ref[...]).astype(BF16), wq_ref[...])
    for h in range(N_HEADS_B):
        lo = h * MLA_HEAD_K
        q_ref[:, lo:lo + QK_NOPE] = (q[:, lo:lo + QK_NOPE] * c).astype(q_ref.dtype)
        rot = _rope_lanes(q[:, lo + QK_NOPE:lo + MLA_HEAD_K], cos, sin, quarter)
        q_ref[:, lo + QK_NOPE:lo + MLA_HEAD_K] = (rot * c).astype(q_ref.dtype)
    ckv = _rms(z_ref[:, Q_LORA:Q_LORA + KV_LORA], gkv_ref[...])
    ckv_ref[...] = ckv
    kr = z_ref[:, Q_LORA + KV_LORA:]
    kr_ref[...] = kr
    xa_ref[:, :KV_LORA] = ckv.astype(xa_ref.dtype)
    kr_rot = _rope_lanes(jnp.concatenate([kr, kr], axis=1), cos, sin, quarter)
    xa_ref[:, KV_LORA:] = kr_rot[:, :QK_ROPE].astype(xa_ref.dtype)


def _mla_prep(z, gq, gkv, wq_cat, cos, sin):
    tm = TM
    zc = Q_LORA + KV_LORA + QK_ROPE
    n_p = TP // tm
    per_b = DEC_SEQ // tm
    full = lambda shape: pl.BlockSpec(shape, lambda i: (0, 0))
    rows = lambda c: pl.BlockSpec((tm, c), lambda i: (i, 0))
    tab = pl.BlockSpec((tm, LANES), lambda i: (jnp.where(i < n_p, 0, 1 + (i - n_p) % per_b), 0))
    return pl.pallas_call(
        _mla_prep_kernel,
        grid=(T // tm,),
        in_specs=[rows(zc), full((1, Q_LORA)), full((1, KV_LORA)), full((Q_LORA, N_HEADS_B * MLA_HEAD_K)), tab, tab],
        out_specs=[rows(N_HEADS_B * MLA_HEAD_K), rows(KV_LORA), rows(QK_ROPE), rows(MLA_AUG)],
        out_shape=[jax.ShapeDtypeStruct((T, N_HEADS_B * MLA_HEAD_K), BF16),
                   jax.ShapeDtypeStruct((T, KV_LORA), F32),
                   jax.ShapeDtypeStruct((T, QK_ROPE), F32),
                   jax.ShapeDtypeStruct((T, MLA_AUG), BF16)],
        compiler_params=_cparams(("parallel",)),
        name="mla_prep",
    )(z, gq.reshape(1, Q_LORA), gkv.reshape(1, KV_LORA), wq_cat, cos, sin)


def _mla_weights(w_uq, w_ukv):
    wq = w_uq.reshape(Q_LORA, N_HEADS_B, QK_NOPE + QK_ROPE)
    wq = jnp.pad(wq, ((0, 0), (0, 0), (0, MLA_HEAD_K - QK_NOPE - QK_ROPE))).reshape(Q_LORA, N_HEADS_B * MLA_HEAD_K)
    wkv = w_ukv.reshape(KV_LORA, N_HEADS_B, QK_NOPE + V_HD)
    wk = jnp.pad(wkv[:, :, :QK_NOPE], ((0, QK_ROPE), (0, 0), (0, MLA_HEAD_K - QK_NOPE)))
    eye = jnp.pad(jnp.eye(QK_ROPE, dtype=F32), ((KV_LORA, 0), (QK_NOPE, MLA_HEAD_K - QK_NOPE - QK_ROPE)))
    wk = (wk + eye[:, None, :]).reshape(MLA_AUG, N_HEADS_B * MLA_HEAD_K)
    wv = jnp.pad(wkv[:, :, QK_NOPE:], ((0, QK_ROPE), (0, 0), (0, 0))).reshape(MLA_AUG, B_W)
    return wq.astype(BF16), jnp.concatenate([wk, wv], axis=1).astype(BF16)


def _mla_kernel(q_ref, k_ref, v_ref, o_ref):
    for h in range(N_HEADS_B):
        ks = slice(h * MLA_HEAD_K, (h + 1) * MLA_HEAD_K)
        vs = slice(h * V_HD, (h + 1) * V_HD)
        s = _dot_nt(q_ref[:, ks], k_ref[:, ks])
        p = jnp.exp2(s - jnp.max(s, axis=-1, keepdims=True))
        denom = jnp.sum(p, axis=-1, keepdims=True)
        o_ref[:, vs] = (_dot(p.astype(BF16), v_ref[:, vs]) / denom).astype(o_ref.dtype)


def _mla_attend(q, kv, tq, q_base, n_seq, nq, nk, name):
    kw = N_HEADS_B * MLA_HEAD_K
    return pl.pallas_call(
        _mla_kernel,
        grid=(n_seq, nq),
        in_specs=[pl.BlockSpec((tq, kw), lambda b, i: (q_base + b * nq + i, 0)),
                  pl.BlockSpec((nk, kw), lambda b, i: (b, 0), pipeline_mode=pl.Buffered(1)),
                  pl.BlockSpec((nk, B_W), lambda b, i: (b, kw // B_W), pipeline_mode=pl.Buffered(1))],
        out_specs=pl.BlockSpec((tq, B_W), lambda b, i: (b * nq + i, 0)),
        out_shape=jax.ShapeDtypeStruct((n_seq * nq * tq, B_W), BF16),
        compiler_params=_cparams(("parallel", "arbitrary")),
        name=name,
    )(q, kv, kv)


def _pool_kernel(u_ref, w_ref, s_ref, o_ref, pad_ref, *, n):
    chunk = min(n, 256)
    zeros = jnp.zeros((POOL_HALO, C_W), F32)
    pad_ref[0:POOL_HALO, :] = zeros
    pad_ref[POOL_HALO + n:POOL_HALO + n + POOL_HALO, :] = zeros
    pad_ref[POOL_HALO:POOL_HALO + n, :] = u_ref[...]
    for c in range(n // chunk):
        r0 = c * chunk
        t = lax.broadcasted_iota(jnp.int32, (chunk, 1), 0) + r0
        for g, win in enumerate(POOL_WINDOWS):
            left = win // 2
            right = win - left - 1
            cols = slice(g * POOL_GW, (g + 1) * POOL_GW)
            acc = pad_ref[POOL_HALO + r0 - left:POOL_HALO + r0 - left + chunk, cols]
            for k in range(-left + 1, right + 1):
                acc = acc + pad_ref[POOL_HALO + r0 + k:POOL_HALO + r0 + k + chunk, cols]
            cnt = (jnp.minimum(t + right, n - 1) + 1 - jnp.maximum(t - left, 0)).astype(F32)
            d = acc / cnt - pad_ref[POOL_HALO + r0:POOL_HALO + r0 + chunk, cols]
            y = _dot(d.astype(BF16), w_ref[g]) * s_ref[:, cols]
            o_ref[r0:r0 + chunk, cols] = y.astype(o_ref.dtype)


def _pool(u, w, scale, n, n_seq, row_base, name):
    kern = functools.partial(_pool_kernel, n=n)
    return pl.pallas_call(
        kern,
        grid=(n_seq,),
        in_specs=[pl.BlockSpec((n, C_W), lambda b: (row_base + b, 0)),
                  pl.BlockSpec((POOL_GROUPS, POOL_GW, POOL_GW), lambda b: (0, 0, 0)),
                  pl.BlockSpec((1, C_W), lambda b: (0, 0))],
        out_specs=pl.BlockSpec((n, C_W), lambda b: (b, 0)),
        out_shape=jax.ShapeDtypeStruct((n_seq * n, C_W), BF16),
        scratch_shapes=[pltpu.VMEM((n + 2 * POOL_HALO, C_W), F32)],
        compiler_params=_cparams(("parallel",)),
        name=name,
    )(u, w, scale.reshape(1, C_W))


def _merge_kernel(oap_ref, obp_ref, ocp_ref, oas_ref, obs_ref, ocs_ref, ga_ref, gb_ref, gc_ref,
                  wa_ref, wb_ref, wc_ref, o_ref, *, n_prompt_tiles):
    is_p = pl.program_id(0) < n_prompt_tiles
    pick = lambda p_ref, s_ref: jnp.where(is_p, p_ref[...], s_ref[...])
    y = jax.nn.sigmoid(ga_ref[...].astype(F32)) * _dot(pick(oap_ref, oas_ref), wa_ref[...])
    y = y + jax.nn.sigmoid(gb_ref[...].astype(F32)) * _dot(pick(obp_ref, obs_ref), wb_ref[...])
    y = y + jax.nn.sigmoid(gc_ref[...].astype(F32)) * _dot(pick(ocp_ref, ocs_ref), wc_ref[...])
    o_ref[...] = y.astype(o_ref.dtype)


def _merge(branches_p, branches_s, gates, wa, wb, wc):
    tm, tn = TM, 1024
    nn = D_MODEL // tn
    n_p = TP // tm
    rows_p = pl.BlockSpec((tm, A_W), lambda i, j: (jnp.minimum(i, n_p - 1), 0), pipeline_mode=pl.Buffered(1))
    rows_s = pl.BlockSpec((tm, A_W), lambda i, j: (jnp.maximum(i - n_p, 0), 0))
    gate = lambda k: pl.BlockSpec((tm, tn), lambda i, j: (i, k * nn + j))
    wspec = pl.BlockSpec((A_W, tn), lambda i, j: (0, j))
    return pl.pallas_call(
        functools.partial(_merge_kernel, n_prompt_tiles=n_p),
        grid=(T // tm, nn),
        in_specs=[rows_p] * 3 + [rows_s] * 3 + [gate(0), gate(1), gate(2), wspec, wspec, wspec],
        out_specs=pl.BlockSpec((tm, tn), lambda i, j: (i, j)),
        out_shape=jax.ShapeDtypeStruct((T, D_MODEL), BF16),
        compiler_params=_cparams(("parallel", "arbitrary")),
        name="merge",
    )(*branches_p, *branches_s, gates, gates, gates, wa, wb, wc)


def _out_proj_kernel(*refs, n_x, n_prompt_tiles):
    y_ref, w_ref = refs[:2]
    x_refs = refs[2:2 + n_x]
    gate_ref, g_ref, sh_ref, sc_ref, x1_ref = refs[2 + n_x:7 + n_x]
    h_refs = refs[7 + n_x:]
    x = x_refs[0][...] if n_x == 1 else _pick_rows(x_refs[0], x_refs[1], n_prompt_tiles)
    x1 = x + gate_ref[...] * _dot(y_ref[...], w_ref[...])
    x1_ref[...] = x1
    if h_refs:
        h_refs[0][...] = (_rms(x1, g_ref[...]) * (1 + sc_ref[...]) + sh_ref[...]).astype(h_refs[0].dtype)


def _out_proj(y, w, x, mods, g, emit_h):
    tm = 512
    rows = pl.BlockSpec((tm, D_MODEL), lambda i: (i, 0))
    mod = lambda k: pl.BlockSpec((None, 1, D_MODEL), lambda i: (_cond_row(i, tm), 0, k))
    if isinstance(x, tuple):
        x_specs, n_p = _split_row_specs(tm, D_MODEL)
    else:
        x, x_specs, n_p = (x,), [rows], 0
    n_out = 2 if emit_h else 1
    return pl.pallas_call(
        functools.partial(_out_proj_kernel, n_x=len(x), n_prompt_tiles=n_p),
        grid=(T // tm,),
        in_specs=[rows, pl.BlockSpec((D_MODEL, D_MODEL), lambda i: (0, 0))] + x_specs + [
            mod(2), pl.BlockSpec((1, D_MODEL), lambda i: (0, 0)), mod(3), mod(4)],
        out_specs=[rows, rows][:n_out],
        out_shape=[jax.ShapeDtypeStruct((T, D_MODEL), F32), jax.ShapeDtypeStruct((T, D_MODEL), BF16)][:n_out],
        compiler_params=_cparams(("parallel",)),
        name="out_proj",
    )(y, w, *x, mods, g.reshape(1, D_MODEL), mods, mods)


def _swiglu_accumulate(x, wg_ref, wu_ref, wd_ref, o_ref, j):
    g = _dot(x, wg_ref[...].astype(BF16))
    u = _dot(x, wu_ref[...].astype(BF16))
    h = (g * jax.nn.sigmoid(g) * u).astype(BF16)

    @pl.when(j == 0)
    def _():
        o_ref[...] = _dot(h, wd_ref[...].astype(BF16))

    @pl.when(j > 0)
    def _():
        o_ref[...] += _dot(h, wd_ref[...].astype(BF16))


def _ffn_dense_kernel(x_ref, wg_ref, wu_ref, wd_ref, o_ref):
    _swiglu_accumulate(x_ref[...], wg_ref, wu_ref, wd_ref, o_ref, pl.program_id(1))


def _ffn_dense(h, wg, wu, wd):
    tm = TM_FFN
    return pl.pallas_call(
        _ffn_dense_kernel,
        grid=(T // tm, NF),
        in_specs=[pl.BlockSpec((tm, D_MODEL), lambda i, j: (i, 0)),
                  pl.BlockSpec((D_MODEL, TF), lambda i, j: (0, j)),
                  pl.BlockSpec((D_MODEL, TF), lambda i, j: (0, j)),
                  pl.BlockSpec((TF, D_MODEL), lambda i, j: (j, 0))],
        out_specs=pl.BlockSpec((tm, D_MODEL), lambda i, j: (i, 0)),
        out_shape=jax.ShapeDtypeStruct((T, D_MODEL), F32),
        compiler_params=_cparams(("parallel", "arbitrary")),
        name="ffn_dense",
    )(h, wg, wu, wd)


def _norm_router_kernel(x_ref, g_ref, sh_ref, sc_ref, whi_ref, wlo_ref, h_ref, meta_ref, wt_ref, cnt_ref, carry_ref):
    i = pl.program_id(0)

    @pl.when(i == 0)
    def _():
        carry_ref[...] = jnp.zeros_like(carry_ref)

    h = _rms(x_ref[...], g_ref[...]) * (1 + sc_ref[...]) + sh_ref[...]
    h_ref[...] = h
    hi = h.astype(BF16)
    lo = (h - hi.astype(F32)).astype(BF16)
    logits = _dot(hi, whi_ref[...]) + _dot(lo, whi_ref[...]) + _dot(hi, wlo_ref[...])
    lane = lax.broadcasted_iota(jnp.int32, logits.shape, 1)
    lg = jnp.where(lane < N_EXPERTS, logits, -jnp.inf)
    v1 = jnp.max(lg, axis=-1, keepdims=True)
    i1 = jnp.min(jnp.where(lg == v1, lane, LANES), axis=-1, keepdims=True)
    lg2 = jnp.where(lane == i1, -jnp.inf, lg)
    v2 = jnp.max(lg2, axis=-1, keepdims=True)
    i2 = jnp.min(jnp.where(lg2 == v2, lane, LANES), axis=-1, keepdims=True)
    e2 = jnp.exp(v2 - v1)
    tot = 1.0 + e2
    wt_ref[...] = jnp.where(lane == 0, 1.0 / tot, jnp.where(lane == 1, e2 / tot, 0.0))
    oh1 = (lane == i1).astype(F32)
    oh2 = (lane == i2).astype(F32)
    r = lax.broadcasted_iota(jnp.int32, (TR, TR), 0)
    c = lax.broadcasted_iota(jnp.int32, (TR, TR), 1)
    earlier = (r > c).astype(BF16)
    base = carry_ref[0:1, :]
    c1 = jnp.sum(oh1, axis=0, keepdims=True)
    c2 = jnp.sum(oh2, axis=0, keepdims=True)
    r1 = jnp.sum(oh1 * (base + _dot(earlier, oh1.astype(BF16))), axis=-1, keepdims=True)
    r2 = jnp.sum(oh2 * (base + c1 + _dot(earlier, oh2.astype(BF16))), axis=-1, keepdims=True)
    total = jnp.broadcast_to(base + c1 + c2, carry_ref.shape)
    carry_ref[...] = total
    cnt_ref[...] = total
    meta_ref[...] = jnp.where(lane == 0, i1, jnp.where(lane == 1, i2, jnp.where(
        lane == 2, r1.astype(jnp.int32), jnp.where(lane == 3, r2.astype(jnp.int32), 0))))


def _norm_router(x, g, mods, w_hi, w_lo):
    rows = lambda c: pl.BlockSpec((TR, c), lambda i: (i, 0))
    full = lambda shape: pl.BlockSpec(shape, lambda i: (0, 0))
    mod = lambda k: pl.BlockSpec((None, 1, D_MODEL), lambda i: (_cond_row(i, TR), 0, k))
    return pl.pallas_call(
        _norm_router_kernel,
        grid=(T // TR,),
        in_specs=[rows(D_MODEL), full((1, D_MODEL)), mod(3), mod(4), full((D_MODEL, LANES)), full((D_MODEL, LANES))],
        out_specs=[rows(D_MODEL), rows(LANES), rows(LANES), full((8, LANES))],
        out_shape=[jax.ShapeDtypeStruct((T, D_MODEL), F32),
                   jax.ShapeDtypeStruct((T, LANES), jnp.int32),
                   jax.ShapeDtypeStruct((T, LANES), F32),
                   jax.ShapeDtypeStruct((8, LANES), F32)],
        scratch_shapes=[pltpu.VMEM((8, LANES), F32)],
        compiler_params=_cparams(("arbitrary",)),
        name="norm_router",
    )(x, g.reshape(1, D_MODEL), mods, mods, w_hi, w_lo)


def _dispatch(meta, counts):
    experts = jnp.arange(N_EXPERTS, dtype=jnp.int32)
    cnt = counts[0, :N_EXPERTS].astype(jnp.int32)
    padded = ((cnt + TM_FFN - 1) // TM_FFN) * TM_FFN
    g_end = jnp.sum(jnp.where(experts[None, :] <= experts[:, None], padded[None, :], 0), axis=1)
    g_start = g_end - padded
    e = meta[:, :TOP_K]
    start_of = jnp.sum(jnp.where(e[:, :, None] == experts[None, None, :], g_start[None, None, :], 0), axis=-1)
    slot = start_of + meta[:, TOP_K:2 * TOP_K]
    end_tiles = g_end // TM_FFN
    n_used = end_tiles[N_EXPERTS - 1]
    tiles = jnp.arange(MOE_TILES, dtype=jnp.int32)
    owner = lambda t: jnp.minimum(jnp.sum((end_tiles[None, :] <= t[:, None]).astype(jnp.int32), axis=1), N_EXPERTS - 1)
    tile_e = owner(jnp.minimum(tiles, n_used - 1))
    return slot, g_end, padded, tile_e, n_used.reshape(1)


def _row_copy(src, src_row, dst, dst_row, sem):
    return pltpu.make_async_copy(src.at[pl.ds(src_row, 1), :], dst.at[pl.ds(dst_row, 1), :], sem)


def _moe_scatter_kernel(ge_ref, pd_ref, slot_hbm, h_ref, xs_hbm, slot_smem, zero_ref, sem_ids, sem_rows, sem_zero):
    i = pl.program_id(0)

    def clear_copies(first_row):
        base = pl.multiple_of(first_row, ZR)
        return [pltpu.make_async_copy(zero_ref, xs_hbm.at[pl.ds(base + k * ZR, ZR), :], sem_zero)
                for k in range(TM_FFN // ZR)]

    def for_unused_tiles(fn):
        def body(t, carry):
            for cp in clear_copies(t * TM_FFN):
                fn(cp)
            return carry
        lax.fori_loop(ge_ref[N_EXPERTS - 1] // TM_FFN, MOE_TILES, body, 0)

    @pl.when(i == 0)
    def _():
        zero_ref[...] = jnp.zeros_like(zero_ref)
        for e in range(N_EXPERTS):
            @pl.when(pd_ref[e] > 0)
            def _():
                for cp in clear_copies(ge_ref[e] - TM_FFN):
                    cp.start()
        for_unused_tiles(lambda cp: cp.start())
        for e in range(N_EXPERTS):
            @pl.when(pd_ref[e] > 0)
            def _():
                for cp in clear_copies(ge_ref[e] - TM_FFN):
                    cp.wait()
        for_unused_tiles(lambda cp: cp.wait())

    ids_copy = pltpu.make_async_copy(slot_hbm.at[i], slot_smem, sem_ids)
    ids_copy.start()
    ids_copy.wait()

    def start(r, carry):
        _row_copy(h_ref, r, xs_hbm, slot_smem[TOP_K * r], sem_rows).start(priority=0)
        _row_copy(h_ref, r, xs_hbm, slot_smem[TOP_K * r + 1], sem_rows).start(priority=1)
        return carry

    lax.fori_loop(0, TR, start, 0)

    def wait(r, carry):
        _row_copy(h_ref, r, xs_hbm, 0, sem_rows).wait()
        _row_copy(h_ref, r, xs_hbm, 0, sem_rows).wait()
        return carry

    lax.fori_loop(0, TR, wait, 0)


def _moe_scatter(h, slot, g_end, padded):
    grid_spec = pltpu.PrefetchScalarGridSpec(
        num_scalar_prefetch=2,
        grid=(T // TR,),
        in_specs=[pl.BlockSpec(memory_space=pl.ANY),
                  pl.BlockSpec((TR, D_MODEL), lambda i, ge, pd: (i, 0))],
        out_specs=pl.BlockSpec(memory_space=pl.ANY),
        scratch_shapes=[pltpu.SMEM((TOP_K * TR,), jnp.int32),
                        pltpu.VMEM((ZR, D_MODEL), F32),
                        pltpu.SemaphoreType.DMA,
                        pltpu.SemaphoreType.DMA,
                        pltpu.SemaphoreType.DMA],
    )
    return pl.pallas_call(
        _moe_scatter_kernel,
        grid_spec=grid_spec,
        out_shape=jax.ShapeDtypeStruct((MOE_TILES * TM_FFN, D_MODEL), F32),
        compiler_params=_cparams(("arbitrary",)),
        name="moe_scatter",
    )(g_end, padded, slot.reshape(T // TR, TOP_K * TR), h)


def _ffn_moe_kernel(te_ref, nu_ref, xs_hbm, wg_ref, wu_ref, wd_ref, o_ref, xg_ref, xb_ref, sem):
    i = pl.program_id(0)
    j = pl.program_id(1)
    n_used = nu_ref[0]
    used = i < n_used

    def tile_copy(t):
        return pltpu.make_async_copy(xs_hbm.at[pl.ds(pl.multiple_of(t * TM_FFN, TM_FFN), TM_FFN), :], xg_ref, sem)

    @pl.when(used & (j == 0))
    def _():
        @pl.when(i == 0)
        def _():
            tile_copy(0).start()

        tile_copy(i).wait()
        xb_ref[...] = xg_ref[...].astype(BF16)

        @pl.when(i + 1 < n_used)
        def _():
            tile_copy(i + 1).start()

    @pl.when(used)
    def _():
        _swiglu_accumulate(xb_ref[...], wg_ref, wu_ref, wd_ref, o_ref, j)

    @pl.when(jnp.logical_not(used) & (j == 0))
    def _():
        o_ref[...] = jnp.zeros_like(o_ref)


def _ffn_moe(xs, tile_e, n_used, wg, wu, wd):
    nf = D_FF // TF_MOE

    def hid(i, j, te, nu):
        return jnp.where(i < nu[0], j, nf - 1)

    grid_spec = pltpu.PrefetchScalarGridSpec(
        num_scalar_prefetch=2,
        grid=(MOE_TILES, nf),
        in_specs=[pl.BlockSpec(memory_space=pl.ANY),
                  pl.BlockSpec((None, None, D_MODEL, TF_MOE), lambda i, j, te, nu: (0, te[i], 0, hid(i, j, te, nu))),
                  pl.BlockSpec((None, None, D_MODEL, TF_MOE), lambda i, j, te, nu: (0, te[i], 0, hid(i, j, te, nu))),
                  pl.BlockSpec((None, None, TF_MOE, D_MODEL), lambda i, j, te, nu: (0, te[i], hid(i, j, te, nu), 0))],
        out_specs=pl.BlockSpec((TM_FFN, D_MODEL), lambda i, j, te, nu: (i, 0)),
        scratch_shapes=[pltpu.VMEM((TM_FFN, D_MODEL), F32),
                        pltpu.VMEM((TM_FFN, D_MODEL), BF16),
                        pltpu.SemaphoreType.DMA],
    )
    return pl.pallas_call(
        _ffn_moe_kernel,
        grid_spec=grid_spec,
        out_shape=jax.ShapeDtypeStruct((MOE_TILES * TM_FFN, D_MODEL), F32),
        compiler_params=_cparams(("arbitrary", "arbitrary")),
        name="ffn_moe",
    )(tile_e, n_used, xs, wg, wu, wd)


def _combine_kernel(slot_hbm, ys_hbm, wt_ref, x_ref, gate_ref, g_ref, yp_ref, ysm_ref,
                    slot_smem, y0_ref, y1_ref, sem_ids, sem_rows, *, n_prompt_tiles):
    i = pl.program_id(0)
    ids_copy = pltpu.make_async_copy(slot_hbm.at[i], slot_smem, sem_ids)
    ids_copy.start()
    ids_copy.wait()

    def start(r, carry):
        _row_copy(ys_hbm, slot_smem[TOP_K * r], y0_ref, r, sem_rows).start(priority=0)
        _row_copy(ys_hbm, slot_smem[TOP_K * r + 1], y1_ref, r, sem_rows).start(priority=1)
        return carry

    lax.fori_loop(0, TC, start, 0)

    def wait(r, carry):
        _row_copy(ys_hbm, 0, y0_ref, r, sem_rows).wait()
        _row_copy(ys_hbm, 0, y1_ref, r, sem_rows).wait()
        return carry

    lax.fori_loop(0, TC, wait, 0)
    wt = wt_ref[...]
    y = wt[:, 0:1] * y0_ref[...] + wt[:, 1:2] * y1_ref[...]
    out = _rms(x_ref[...] + gate_ref[...] * y, g_ref[...])

    @pl.when(i < n_prompt_tiles)
    def _():
        yp_ref[...] = out

    @pl.when(i >= n_prompt_tiles)
    def _():
        ysm_ref[...] = out


def _combine(slot, ys, wt, x, mods, k_gate, g):
    n_p = TP // TC
    return pl.pallas_call(
        functools.partial(_combine_kernel, n_prompt_tiles=n_p),
        grid=(T // TC,),
        in_specs=[pl.BlockSpec(memory_space=pl.ANY),
                  pl.BlockSpec(memory_space=pl.ANY),
                  pl.BlockSpec((TC, LANES), lambda i: (i, 0)),
                  pl.BlockSpec((TC, D_MODEL), lambda i: (i, 0)),
                  pl.BlockSpec((None, 1, D_MODEL), lambda i: (_cond_row(i, TC), 0, k_gate)),
                  pl.BlockSpec((1, D_MODEL), lambda i: (0, 0))],
        out_specs=[pl.BlockSpec((TC, D_MODEL), lambda i: (jnp.minimum(i, n_p - 1), 0)),
                   pl.BlockSpec((TC, D_MODEL), lambda i: (jnp.maximum(i - n_p, 0), 0))],
        out_shape=[jax.ShapeDtypeStruct((TP, D_MODEL), F32), jax.ShapeDtypeStruct((TS, D_MODEL), F32)],
        scratch_shapes=[pltpu.SMEM((TOP_K * TC,), jnp.int32),
                        pltpu.VMEM((TC, D_MODEL), F32),
                        pltpu.VMEM((TC, D_MODEL), F32),
                        pltpu.SemaphoreType.DMA,
                        pltpu.SemaphoreType.DMA],
        compiler_params=_cparams(("arbitrary",)),
        name="moe_combine",
    )(slot.reshape(T // TC, TOP_K * TC), ys, wt, x, mods, g.reshape(1, D_MODEL))


def kernel(x_prompt, x_sample, cache_attn_k, cache_attn_v, cache_mla_ckv, cache_mla_krope, c, c_ctx, ln1_g, ln2_g, w_ada, b_ada, w_in, attn_sink, mla_q_norm_g, w_uq, mla_kv_norm_g, w_ukv, pool_w, pool_scale, w_branch_a, w_branch_b, w_branch_c, w_out, ffn_w_gate, ffn_w_up, ffn_w_down, router_w, moe_w_gate, moe_w_up, moe_w_down, final_g):
    x = (x_prompt.reshape(TP, D_MODEL), x_sample.reshape(TS, D_MODEL))
    w_in_t = jnp.swapaxes(w_in, 1, 2)
    cond =jnp.concatenate([c_ctx[None, :], c, jnp.zeros((N_COND - 1 - DEC_BATCH, D_MODEL), F32)], axis=0)
    rows = DEC_SEQ // GRID_W
    cos_a, sin_a = _rope_tables(rows, HD_A)
    cos_b, sin_b = _rope_tables(rows, QK_ROPE)
    st_k, st_v, st_ckv, st_kr = [], [], [], []
    assert DEPTH == 2, "layer 0 is the dense-FFN layer, layer 1 the expert layer that ends the trunk"
    all_mods = [_adaln(cond, w_ada, b_ada, l).reshape(N_COND, 1, 6 * D_MODEL) for l in range(DEPTH)]
    cos_t = jnp.concatenate([jnp.ones((TM, LANES), F32), cos_b], axis=0)
    sin_t = jnp.concatenate([jnp.zeros((TM, LANES), F32), sin_b], axis=0)
    h1 = _norm_mod(x[0], x[1], ln1_g[0], all_mods[0], 0, 1)
    for l in range(DEPTH):
        mods = all_mods[l]
        qa = _matmul_wt(h1, w_in_t, l, 0, A_W, A_W, F32, "proj_q")
        kv = _matmul_wt(h1, w_in_t, l, OFF_K, 2 * KV_W, 2 * KV_W, F32, "proj_kv")
        zb = _matmul_wt(h1, w_in_t, l, OFF_CQ, OFF_POOL - OFF_CQ, OFF_POOL - OFF_CQ, F32, "proj_mla")
        u = _matmul_wt(h1, w_in_t, l, OFF_POOL, C_W, C_W, F32, "proj_pool")
        gates = _matmul_wt(h1, w_in_t, l, OFF_GATE, 3 * D_MODEL, 1024, BF16, "proj_gates")
        sink = attn_sink[l]
        ck = cache_attn_k[:, l].reshape(DEC_BATCH, PAST_LEN, KV_W)
        cv = cache_attn_v[:, l].reshape(DEC_BATCH, PAST_LEN, KV_W)
        oa = (_attn_a_prompt(qa, kv, sink), _attn_a_sample(qa, kv, ck, cv, sink, cos_a, sin_a))
        wq_cat, w_aug = _mla_weights(w_uq[l], w_ukv[l])
        q, ckv, kr, xa = _mla_prep(zb, mla_q_norm_g[l], mla_kv_norm_g[l], wq_cat, cos_t, sin_t)
        cache_aug = jnp.concatenate([cache_mla_ckv[:, l], cache_mla_krope[:, l]], axis=-1).astype(BF16)
        xa_s = jnp.concatenate([xa[TP:].reshape(DEC_BATCH, DEC_SEQ, MLA_AUG), cache_aug], axis=1)
        kv_p = _matmul(xa[:TP], w_aug, BF16, TM, B_W, "mla_kv_prompt")
        kv_s = _matmul(xa_s.reshape(DEC_BATCH * MLA_KEYS, MLA_AUG), w_aug, BF16, TM, B_W, "mla_kv_sample")
        ob = (_mla_attend(q, kv_p, SEQ, 0, BATCH, 1, SEQ, "mla_prompt"),
              _mla_attend(q, kv_s, TQ_MLA, TP // TQ_MLA, DEC_BATCH, DEC_SEQ // TQ_MLA, MLA_KEYS, "mla_sample"))
        pw = pool_w[l].astype(BF16)
        oc = (_pool(u, pw, pool_scale[l], SEQ, BATCH, 0, "pool_prompt"),
              _pool(u, pw, pool_scale[l], DEC_SEQ, DEC_BATCH, TP // DEC_SEQ, "pool_sample"))
        y = _merge((oa[0], ob[0], oc[0]), (oa[1], ob[1], oc[1]), gates, w_branch_a[l].astype(BF16),
                   w_branch_b[l].astype(BF16), w_branch_c[l].astype(BF16))
        st_k.append(kv[:TP, :KV_W].reshape(BATCH, SEQ, N_KV_A, HD_A))
        st_v.append(kv[:TP, KV_W:].reshape(BATCH, SEQ, N_KV_A, HD_A))
        st_ckv.append(ckv[:TP].reshape(BATCH, SEQ, KV_LORA))
        st_kr.append(kr[:TP].reshape(BATCH, SEQ, QK_ROPE))
        if l == 0:
            x, h2 = _out_proj(y, w_out[l].astype(BF16), x, mods, ln2_g[l], True)
            f = _ffn_dense(h2, ffn_w_gate[0].astype(BF16), ffn_w_up[0].astype(BF16), ffn_w_down[0].astype(BF16))
            x, h1 = _resid_norm(x, f, mods, 5, ln1_g[1], all_mods[1], 0, 1)
        else:
            (x,) = _out_proj(y, w_out[l].astype(BF16), x, mods, ln2_g[l], False)
            rw = jnp.pad(router_w[0], ((0, 0), (0, LANES - N_EXPERTS)))
            rw_hi = rw.astype(BF16)
            rw_lo = (rw - rw_hi.astype(F32)).astype(BF16)
            h2, meta, wt, counts = _norm_router(x, ln2_g[l], mods, rw_hi, rw_lo)
            slot, g_end, padded, tile_e, n_used = _dispatch(meta, counts)
            xs = _moe_scatter(h2, slot, g_end, padded)
            ys = _ffn_moe(xs, tile_e, n_used, moe_w_gate, moe_w_up, moe_w_down)
            y_p, y_s = _combine(slot, ys, wt, x, mods, 5, final_g)
    return (y_p.reshape(BATCH, SEQ, D_MODEL), y_s.reshape(DEC_BATCH, DEC_SEQ, D_MODEL),
            jnp.stack(st_k, axis=1), jnp.stack(st_v, axis=1), jnp.stack(st_ckv, axis=1), jnp.stack(st_kr, axis=1))
```
